```python
import functools
import jax, jax.numpy as jnp
from jax import lax
import numpy as np

D_MODEL = 1024
BATCH = 1
SEQ = 16384
DEPTH = 1
DEC_BATCH = 128
DEC_SEQ = 8
PAST_LEN = 16384
PAGE_SIZE = 128

HEAD_DIM = 64
N_HEADS = 8
N_KV_HEADS = 2
GROUP = N_HEADS // N_KV_HEADS
ATT_WIDTH = N_HEADS * HEAD_DIM
KV_WIDTH = N_KV_HEADS * HEAD_DIM
WINDOW = 128
BLOCK = 128
ROPE_THETA = 10000.0
C_CONV = D_MODEL // 2
CONV_K = 31
D_FF = 11 * D_MODEL // 4
FFN_K = 3
PLE_DIM = 256
SPLIT_SIZES = (ATT_WIDTH, KV_WIDTH, KV_WIDTH, C_CONV, C_CONV, D_MODEL, D_MODEL)
SPLIT_POINTS = tuple(int(v) for v in np.cumsum(SPLIT_SIZES)[:-1])
IN_COLS = sum(SPLIT_SIZES)
DN_ALPHA = (2 * DEPTH) ** 0.25
DN_BETA = (8 * DEPTH) ** -0.25
LN_EPS = 1e-5
NEG = -1e30

kernel_name = 'hybrid_conformer_swa_sink_decoder_step'


def layer_norm(x, g, b):
    xf = x.astype(jnp.float32)
    mu = jnp.mean(xf, axis=-1, keepdims=True)
    var = jnp.mean(jnp.square(xf - mu), axis=-1, keepdims=True)
    y = (xf - mu) * lax.rsqrt(var + LN_EPS) * g.astype(jnp.float32) + b.astype(jnp.float32)
    return y.astype(x.dtype)


def rope(x, pos):
    half = HEAD_DIM // 2
    inv = ROPE_THETA ** (-jnp.arange(half, dtype=jnp.float32) / half)
    ang = pos.astype(jnp.float32)[:, None] * inv[None, :]
    cos = jnp.cos(ang)[None, :, None, :]
    sin = jnp.sin(ang)[None, :, None, :]
    xf = x.astype(jnp.float32)
    x1, x2 = xf[..., :half], xf[..., half:]
    return jnp.concatenate([x1 * cos - x2 * sin, x2 * cos + x1 * sin], axis=-1).astype(x.dtype)


def sink_softmax(s, sinks):
    sink = sinks.astype(jnp.float32).reshape(N_KV_HEADS, GROUP)[:, :, None, None]
    sink = jnp.broadcast_to(sink, s.shape[:-1] + (1,))
    p = jax.nn.softmax(jnp.concatenate([s, sink], axis=-1), axis=-1)
    return p[..., :-1]


def swa_prompt(q, k, v, sinks):
    B, T = q.shape[:2]
    nb = T // BLOCK
    qb = q.astype(jnp.float32).reshape(B, nb, BLOCK, N_KV_HEADS, GROUP, HEAD_DIM)
    kb = k.astype(jnp.float32).reshape(B, nb, BLOCK, N_KV_HEADS, HEAD_DIM)
    vb = v.astype(jnp.float32).reshape(B, nb, BLOCK, N_KV_HEADS, HEAD_DIM)
    pad = ((0, 0), (1, 0), (0, 0), (0, 0), (0, 0))
    kk = jnp.concatenate([jnp.pad(kb, pad)[:, :-1], kb], axis=2)
    vv = jnp.concatenate([jnp.pad(vb, pad)[:, :-1], vb], axis=2)
    blk = jnp.arange(nb)[:, None] * BLOCK
    qpos = blk + jnp.arange(BLOCK)[None, :]
    kpos = blk + jnp.arange(2 * BLOCK)[None, :] - BLOCK
    rel = qpos[:, :, None] - kpos[:, None, :]
    mask = (rel >= 0) & (rel < WINDOW) & (kpos[:, None, :] >= 0)
    s = jnp.einsum('bnqhgd,bnkhd->bnhgqk', qb, kk) * (HEAD_DIM ** -0.5)
    s = jnp.where(mask[None, :, None, None], s, NEG)
    p = sink_softmax(s, sinks)
    o = jnp.einsum('bnhgqk,bnkhd->bnqhgd', p, vv)
    return o.reshape(B, T, ATT_WIDTH).astype(q.dtype)


def attend_prompt(q, k, v, sinks, win):
    return swa_prompt(q, k, v, sinks), k[:, -win:], v[:, -win:]


def attend_sample(q, k, v, k_cache, v_cache, pos_q, sinks):
    B, S = q.shape[:2]
    wc = k_cache.shape[1]
    k_all = jnp.concatenate([k_cache.astype(k.dtype), k], axis=1)
    v_all = jnp.concatenate([v_cache.astype(v.dtype), v], axis=1)
    kpos = jnp.concatenate([PAST_LEN - wc + jnp.arange(wc), pos_q])
    rel = pos_q[:, None] - kpos[None, :]
    mask = (rel >= 0) & (rel < WINDOW)
    qf = q.astype(jnp.float32).reshape(B, S, N_KV_HEADS, GROUP, HEAD_DIM)
    s = jnp.einsum('bqhgd,bkhd->bhgqk', qf, k_all.astype(jnp.float32)) * (HEAD_DIM ** -0.5)
    s = jnp.where(mask[None, None, None], s, NEG)
    p = sink_softmax(s, sinks)
    o = jnp.einsum('bhgqk,bkhd->bqhgd', p, v_all.astype(jnp.float32))
    return o.reshape(B, S, ATT_WIDTH).astype(q.dtype), k_all[:, -wc:], v_all[:, -wc:]


def causal_dwconv(x, past, w, b):
    K = w.shape[0]
    C = x.shape[-1]
    xp = jnp.concatenate([past.astype(x.dtype), x], axis=1)
    y = lax.conv_general_dilated(xp, w.astype(x.dtype)[:, None, :], window_strides=(1,), padding='VALID',
                                 dimension_numbers=('NWC', 'WIO', 'NWC'), feature_group_count=C)
    return y + b.astype(x.dtype), xp[:, -(K - 1):]


def decoder_layer(x, pos, p_l, attend, conv_past, ffn_past, lw):
    B, T = x.shape[:2]
    z = x @ lw['w_in']
    q, k, v, glu_a, glu_b, gate_att, gate_conv = jnp.split(z, SPLIT_POINTS, axis=-1)
    q = rope(q.reshape(B, T, N_HEADS, HEAD_DIM), pos)
    k = rope(k.reshape(B, T, N_KV_HEADS, HEAD_DIM), pos)
    v = v.reshape(B, T, N_KV_HEADS, HEAD_DIM)
    o, k_state, v_state = attend(q, k, v)
    att = o @ lw['w_attn_out']
    u = glu_a * jax.nn.sigmoid(glu_b)
    c, conv_state = causal_dwconv(u, conv_past, lw['w_dw'], lw['b_dw'])
    c = jax.nn.silu(layer_norm(c, lw['conv_ln_g'], lw['conv_ln_b'])) @ lw['w_conv_out']
    mixed = (jax.nn.sigmoid(gate_att) * att + jax.nn.sigmoid(gate_conv) * c) @ lw['w_out']
    x = layer_norm(DN_ALPHA * x + mixed, lw['ln1_g'], lw['ln1_b'])
    h, ffn_state = causal_dwconv(x @ lw['w_up'], ffn_past, lw['w_fconv'], lw['b_fconv'])
    h_g, h_v = jnp.split(h, 2, axis=-1)
    f = (jax.nn.gelu(h_g) * h_v) @ lw['w_down']
    x = layer_norm(DN_ALPHA * x + f, lw['ln2_g'], lw['ln2_b'])
    e = jax.nn.sigmoid(x @ lw['w_ple_gate']) * (p_l @ lw['w_ple_proj'])
    x = layer_norm(DN_ALPHA * x + e, lw['ln3_g'], lw['ln3_b'])
    return x, k_state, v_state, conv_state, ffn_state


def setup_inputs(seed: int = 0) -> dict:
    key = jax.random.key(seed)
    ks = iter(jax.random.split(key, 40))
    win_cache = min(WINDOW, PAST_LEN)

    def nrm(shape, scale):
        return jax.random.normal(next(ks), shape, jnp.float32) * scale

    def gain(n):
        return 1.0 + nrm((DEPTH, n), 0.02)

    w_in = nrm((DEPTH, D_MODEL, IN_COLS), D_MODEL ** -0.5)
    v0 = ATT_WIDTH + KV_WIDTH
    w_in = w_in.at[:, :, v0:v0 + KV_WIDTH].multiply(DN_BETA)
    return {
        'x_prompt': nrm((BATCH, SEQ, D_MODEL), 1.0),
        'x_sample': nrm((DEC_BATCH, DEC_SEQ, D_MODEL), 1.0),
        'cache_k': nrm((DEPTH, DEC_BATCH, win_cache, N_KV_HEADS, HEAD_DIM), 1.0),
        'cache_v': nrm((DEPTH, DEC_BATCH, win_cache, N_KV_HEADS, HEAD_DIM), DN_BETA),
        'state_conv': nrm((DEPTH, DEC_BATCH, CONV_K - 1, C_CONV), 0.5),
        'state_ffn_conv': nrm((DEPTH, DEC_BATCH, FFN_K - 1, 2 * D_FF), 1.0),
        'p_prompt': nrm((DEPTH, BATCH, SEQ, PLE_DIM), 1.0),
        'p_sample': nrm((DEPTH, DEC_BATCH, DEC_SEQ, PLE_DIM), 1.0),
        'w_in': w_in,
        'sinks': nrm((DEPTH, N_HEADS), 1.0),
        'w_attn_out': nrm((DEPTH, ATT_WIDTH, D_MODEL), ATT_WIDTH ** -0.5 * DN_BETA),
        'w_dw': nrm((DEPTH, CONV_K, C_CONV), CONV_K ** -0.5),
        'b_dw': nrm((DEPTH, C_CONV), 0.02),
        'conv_ln_g': gain(C_CONV),
        'conv_ln_b': nrm((DEPTH, C_CONV), 0.02),
        'w_conv_out': nrm((DEPTH, C_CONV, D_MODEL), C_CONV ** -0.5 * DN_BETA),
        'w_out': nrm((DEPTH, D_MODEL, D_MODEL), D_MODEL ** -0.5 * DN_BETA),
        'ln1_g': gain(D_MODEL),
        'ln1_b': nrm((DEPTH, D_MODEL), 0.02),
        'w_up': nrm((DEPTH, D_MODEL, 2 * D_FF), D_MODEL ** -0.5 * DN_BETA),
        'w_fconv': nrm((DEPTH, FFN_K, 2 * D_FF), FFN_K ** -0.5),
        'b_fconv': nrm((DEPTH, 2 * D_FF), 0.02),
        'w_down': nrm((DEPTH, D_FF, D_MODEL), D_FF ** -0.5 * DN_BETA),
        'ln2_g': gain(D_MODEL),
        'ln2_b': nrm((DEPTH, D_MODEL), 0.02),
        'w_ple_gate': nrm((DEPTH, D_MODEL, D_MODEL), D_MODEL ** -0.5),
        'w_ple_proj': nrm((DEPTH, PLE_DIM, D_MODEL), PLE_DIM ** -0.5 * DN_BETA),
        'ln3_g': gain(D_MODEL),
        'ln3_b': nrm((DEPTH, D_MODEL), 0.02),
    }


def reference(x_prompt, x_sample, cache_k, cache_v, state_conv, state_ffn_conv, p_prompt, p_sample,
              w_in, sinks, w_attn_out, w_dw, b_dw, conv_ln_g, conv_ln_b, w_conv_out, w_out, ln1_g, ln1_b,
              w_up, w_fconv, b_fconv, w_down, ln2_g, ln2_b, w_ple_gate, w_ple_proj, ln3_g, ln3_b):
    win = cache_k.shape[2]
    pos_p = jnp.arange(x_prompt.shape[1])
    pos_s = PAST_LEN + jnp.arange(x_sample.shape[1])
    bp = x_prompt.shape[0]
    yp, ys = x_prompt, x_sample
    kp_l, vp_l, cp_l, fp_l = [], [], [], []
    ks_l, vs_l, cs_l, fs_l = [], [], [], []
    for l in range(DEPTH):
        lw = {'w_in': w_in[l], 'w_attn_out': w_attn_out[l], 'w_dw': w_dw[l], 'b_dw': b_dw[l],
              'conv_ln_g': conv_ln_g[l], 'conv_ln_b': conv_ln_b[l], 'w_conv_out': w_conv_out[l],
              'w_out': w_out[l], 'ln1_g': ln1_g[l], 'ln1_b': ln1_b[l], 'w_up': w_up[l],
              'w_fconv': w_fconv[l], 'b_fconv': b_fconv[l], 'w_down': w_down[l],
              'ln2_g': ln2_g[l], 'ln2_b': ln2_b[l], 'w_ple_gate': w_ple_gate[l],
              'w_ple_proj': w_ple_proj[l], 'ln3_g': ln3_g[l], 'ln3_b': ln3_b[l]}
        conv0 = jnp.zeros((bp, CONV_K - 1, C_CONV), yp.dtype)
        ffn0 = jnp.zeros((bp, FFN_K - 1, 2 * D_FF), yp.dtype)
        att_p = functools.partial(attend_prompt, sinks=sinks[l], win=win)
        yp, kp, vp, cp, fp = decoder_layer(yp, pos_p, p_prompt[l], att_p, conv0, ffn0, lw)
        att_s = functools.partial(attend_sample, k_cache=cache_k[l], v_cache=cache_v[l], pos_q=pos_s, sinks=sinks[l])
        ys, k_s, v_s, c_s, f_s = decoder_layer(ys, pos_s, p_sample[l], att_s, state_conv[l], state_ffn_conv[l], lw)
        kp_l.append(kp); vp_l.append(vp); cp_l.append(cp); fp_l.append(fp)
        ks_l.append(k_s); vs_l.append(v_s); cs_l.append(c_s); fs_l.append(f_s)
    k_win_prompt = jnp.stack(kp_l)
    v_win_prompt = jnp.stack(vp_l)
    conv_prompt = jnp.stack(cp_l)
    ffn_conv_prompt = jnp.stack(fp_l)
    k_win_sample = jnp.stack(ks_l)
    v_win_sample = jnp.stack(vs_l)
    conv_sample = jnp.stack(cs_l)
    ffn_conv_sample = jnp.stack(fs_l)
    return (yp, ys, k_win_prompt, v_win_prompt, conv_prompt, ffn_conv_prompt,
            k_win_sample, v_win_sample, conv_sample, ffn_conv_sample)
```

```python
import functools
import math

import jax
import jax.numpy as jnp
from jax import lax
from jax.experimental import pallas as pl
from jax.experimental.pallas import tpu as pltpu

F32 = jnp.float32
BF16 = jnp.bfloat16

LANES = 128
HEAD_DIM = 64
N_HEADS = 8
N_KV_HEADS = 2
GROUP = N_HEADS // N_KV_HEADS
WINDOW = 128
ROPE_THETA = 10000.0
PAST_LEN = 16384
LN_EPS = 1e-5
NEG = -1e30
GELU_C = math.sqrt(2.0 / math.pi)

TM = 512
SEQ_BLK = 32
ATT_BLK = 16
FF_CHUNK = 256
CONV_ROWS = 128
VMEM_LIMIT = 56 * 1024 * 1024


def _bf(x):
    return x.astype(BF16)


def _mm(a, w):
    return jnp.dot(_bf(a), w, preferred_element_type=F32)


def _mm_t(a, b):
    return lax.dot_general(a, b, (((1,), (1,)), ((), ())), preferred_element_type=F32)


def _sigmoid(x):
    return 0.5 * (jnp.tanh(0.5 * x) + 1.0)


def _gelu(x):
    return 0.5 * x * (1.0 + jnp.tanh(GELU_C * (x + 0.044715 * (x * x * x))))


def _layer_norm(x, g, b):
    mu = jnp.mean(x, axis=-1, keepdims=True)
    d = x - mu
    var = jnp.mean(d * d, axis=-1, keepdims=True)
    return d * lax.rsqrt(var + LN_EPS) * g + b


def _lane_half(shape):
    return (lax.broadcasted_iota(jnp.int32, shape, len(shape) - 1) // HEAD_DIM) % 2


def _rope(x, cos, sin_signed):
    lane = lax.broadcasted_iota(jnp.int32, x.shape, 1)
    first_half = (lane % HEAD_DIM) < (HEAD_DIM // 2)
    partner = jnp.where(first_half, pltpu.roll(x, LANES - HEAD_DIM // 2, 1), pltpu.roll(x, HEAD_DIM // 2, 1))
    return x * cos + partner * sin_signed


def _swap_halves(x):
    return pltpu.roll(x, HEAD_DIM, 1)


def _softmax_with_sink(s, sink):
    m = jnp.maximum(jnp.max(s, axis=-1, keepdims=True), sink)
    e = jnp.exp(s - m)
    den = jnp.sum(e, axis=-1, keepdims=True) + jnp.exp(sink - m)
    return e, 1.0 / den


def _ffn_tail(x1, f, p, ln2_g, ln2_b, wpg, wpp, ln3_g, ln3_b, alpha):
    x2 = _layer_norm(alpha * x1 + f, ln2_g, ln2_b)
    e = _sigmoid(_mm(x2, wpg)) * _mm(p, wpp)
    return _layer_norm(alpha * x2 + e, ln3_g, ln3_b)


def _prompt_mixer_kernel(sinks_ref, x_ref, cos_ref, sin_ref, w_in_ref, w_att_ref, w_dw_ref, b_dw_ref,
                         cg_ref, cb_ref, w_co_ref, w_out_ref, g1_ref, b1_ref,
                         x1_ref, kwin_ref, vwin_ref, cst_ref,
                         kprev, kprev_sw, vprev2, useq, cbuf, *, alpha):
    i = pl.program_id(0)
    last = pl.num_programs(0) - 1
    tail = useq.shape[1] - TM
    n_grp = useq.shape[0]

    @pl.when(i == 0)
    def _():
        kprev[...] = jnp.zeros_like(kprev)
        kprev_sw[...] = jnp.zeros_like(kprev_sw)
        vprev2[...] = jnp.zeros_like(vprev2)
        useq[:, TM:TM + tail, :] = jnp.zeros((n_grp, tail, LANES), F32)

    useq[:, 0:tail, :] = useq[:, TM:TM + tail, :]

    x = x_ref[...]
    xb = _bf(x)
    cos = cos_ref[...]
    sin = sin_ref[...]

    qkv = jnp.dot(xb, w_in_ref[:, 0:768], preferred_element_type=F32)
    scale = HEAD_DIM ** -0.5
    q = [_rope(qkv[:, c * LANES:(c + 1) * LANES], cos, sin) * scale for c in range(4)]
    k = _rope(qkv[:, 512:640], cos, sin)
    v = qkv[:, 640:768]
    kb, kb_sw = _bf(k), _bf(_swap_halves(k))
    vb2 = jnp.concatenate([_bf(v), _bf(_swap_halves(v))], axis=1)

    half_q = _lane_half((WINDOW, LANES))
    row = lax.broadcasted_iota(jnp.int32, (WINDOW, 2 * WINDOW), 0)
    col = lax.broadcasted_iota(jnp.int32, (WINDOW, 2 * WINDOW), 1)
    stack_same = [(c, a) for c in range(4) for a in range(2) if a == c // 2]
    stack_swap = [(c, a) for c in range(4) for a in range(2) if a != c // 2]

    o_blocks = []
    for bi in range(TM // WINDOW):
        r0 = bi * WINDOW
        if bi == 0:
            kp, kp_sw, vp2 = kprev[...], kprev_sw[...], vprev2[...]
            shift = jnp.where(i == 0, 2 * WINDOW, 0)
        else:
            kp, kp_sw, vp2 = kb[r0 - WINDOW:r0], kb_sw[r0 - WINDOW:r0], vb2[r0 - WINDOW:r0]
            shift = 0
        valid = jnp.where(col < WINDOW, col - row - shift - 1, row - (col - WINDOW)) >= 0
        kk = jnp.concatenate([kp, kb[r0:r0 + WINDOW]], axis=0)
        kk_sw = jnp.concatenate([kp_sw, kb_sw[r0:r0 + WINDOW]], axis=0)
        vv2 = jnp.concatenate([vp2, vb2[r0:r0 + WINDOW]], axis=0)
        head_out = {}
        for heads, keys in ((stack_same, kk), (stack_swap, kk_sw)):
            qm = jnp.concatenate(
                [_bf(jnp.where(half_q == a, q[c][r0:r0 + WINDOW], 0.0)) for c, a in heads], axis=0)
            s_all = _mm_t(qm, keys)
            probs, inv = [], []
            for n, (c, a) in enumerate(heads):
                s = jnp.where(valid, s_all[n * WINDOW:(n + 1) * WINDOW], NEG)
                e, r = _softmax_with_sink(s, sinks_ref[2 * c + a])
                probs.append(_bf(e))
                inv.append(r)
            pv = jnp.dot(jnp.concatenate(probs, axis=0), vv2, preferred_element_type=F32)
            for n, (c, a) in enumerate(heads):
                head_out[(c, a)] = pv[n * WINDOW:(n + 1) * WINDOW] * inv[n]
        chunks = []
        for c in range(4):
            h = c // 2
            lo = head_out[(c, 0)][:, (0 if h == 0 else LANES):(LANES if h == 0 else 2 * LANES)]
            hi = head_out[(c, 1)][:, (0 if h == 1 else LANES):(LANES if h == 1 else 2 * LANES)]
            chunks.append(jnp.where(half_q == 0, lo, hi))
        o_blocks.append(jnp.concatenate(chunks, axis=1))
    o = jnp.concatenate(o_blocks, axis=0)
    att = _mm(o, w_att_ref[...])

    kprev[...] = kb[TM - WINDOW:]
    kprev_sw[...] = kb_sw[TM - WINDOW:]
    vprev2[...] = vb2[TM - WINDOW:]

    @pl.when(i == last)
    def _():
        kwin_ref[...] = k[TM - WINDOW:]
        vwin_ref[...] = v[TM - WINDOW:]

    glu = jnp.dot(xb, w_in_ref[:, 768:1792], preferred_element_type=F32)
    u = glu[:, 0:512] * _sigmoid(glu[:, 512:1024])
    for c in range(n_grp):
        useq[c, tail:tail + TM, :] = u[:, c * LANES:(c + 1) * LANES]
    n_taps = w_dw_ref.shape[0]
    first = tail - (n_taps - 1)
    for c in range(n_grp):
        cols = slice(c * LANES, (c + 1) * LANES)
        for r0 in range(0, TM, CONV_ROWS):
            acc = jnp.broadcast_to(b_dw_ref[:, cols], (CONV_ROWS, LANES))
            for j in range(n_taps):
                acc = acc + useq[c, pl.ds(first + j + r0, CONV_ROWS), :] * w_dw_ref[j:j + 1, cols]
            cbuf[r0:r0 + CONV_ROWS, cols] = acc

    @pl.when(i == last)
    def _():
        for c in range(n_grp):
            cst_ref[:, c * LANES:(c + 1) * LANES] = useq[c, pl.ds(TM + first, n_taps - 1), :]

    cn = _layer_norm(cbuf[...], cg_ref[...], cb_ref[...])
    cproj = _mm(cn * _sigmoid(cn), w_co_ref[...])

    gates = jnp.dot(xb, w_in_ref[:, 1792:3840], preferred_element_type=F32)
    merged = _sigmoid(gates[:, 0:1024]) * att + _sigmoid(gates[:, 1024:2048]) * cproj
    mixed = _mm(merged, w_out_ref[...])
    x1_ref[...] = _layer_norm(alpha * x + mixed, g1_ref[...], b1_ref[...])


def _prompt_ffn_kernel(x1_ref, p_ref, w_up_ref, w_fc_ref, b_fc_ref, w_dn_ref, g2_ref, b2_ref,
                       wpg_ref, wpp_ref, g3_ref, b3_ref,
                       y_ref, fst_ref, upseq, *, alpha):
    i = pl.program_id(0)
    last = pl.num_programs(0) - 1
    n_slab = upseq.shape[0]
    tail = upseq.shape[1] - TM
    d_ff = w_dn_ref.shape[0]
    n_taps = w_fc_ref.shape[0]

    @pl.when(i == 0)
    def _():
        upseq[:, TM:TM + tail, :] = jnp.zeros((n_slab, tail, LANES), F32)

    upseq[:, 0:tail, :] = upseq[:, TM:TM + tail, :]

    x1 = x1_ref[...]
    xb = _bf(x1)
    f = jnp.zeros((TM, x1.shape[1]), F32)
    for c0 in range(0, d_ff, FF_CHUNK):
        halves = []
        for base in (c0, d_ff + c0):
            up = jnp.dot(xb, w_up_ref[:, base:base + FF_CHUNK], preferred_element_type=F32)
            parts = []
            for s0 in range(0, FF_CHUNK, LANES):
                slab = (base + s0) // LANES
                cols = slice(base + s0, base + s0 + LANES)
                cur = up[:, s0:s0 + LANES]
                upseq[slab, tail:tail + TM, :] = cur
                hcv = b_fc_ref[:, cols] + cur * w_fc_ref[n_taps - 1:n_taps, cols]
                for j in range(n_taps - 1):
                    hcv = hcv + upseq[slab, pl.ds(tail - (n_taps - 1) + j, TM), :] * w_fc_ref[j:j + 1, cols]
                parts.append(hcv)
            halves.append(jnp.concatenate(parts, axis=1))
        act = _gelu(halves[0]) * halves[1]
        f = f + jnp.dot(_bf(act), w_dn_ref[c0:c0 + FF_CHUNK, :], preferred_element_type=F32)

    @pl.when(i == last)
    def _():
        for s in range(n_slab):
            fst_ref[:, s * LANES:(s + 1) * LANES] = upseq[s, pl.ds(TM + tail - (n_taps - 1), n_taps - 1), :]

    y_ref[...] = _ffn_tail(x1, f, p_ref[...], g2_ref[...], b2_ref[...], wpg_ref[...], wpp_ref[...],
                           g3_ref[...], b3_ref[...], alpha)


def _rows_time_major(ref, width, steps):
    return jnp.concatenate([ref[:, t * width:(t + 1) * width] for t in range(steps)], axis=0)


def _sample_proj_kernel(x_ref, cos_ref, sin_ref, st_ref, w_in_ref, w_dw_ref, b_dw_ref, cg_ref, cb_ref, w_co_ref,
                        q_ref, k_ref, v_ref, cst_ref, ga_ref, ct_ref, *, steps):
    nb = x_ref.shape[0]
    d_model = x_ref.shape[1] // steps
    x = _rows_time_major(x_ref, d_model, steps)
    xb = _bf(x)
    cos = cos_ref[...]
    sin = sin_ref[...]

    qkv = jnp.dot(xb, w_in_ref[:, 0:768], preferred_element_type=F32)
    scale = HEAD_DIM ** -0.5
    q = jnp.concatenate([_rope(qkv[:, c * LANES:(c + 1) * LANES], cos, sin) * scale for c in range(4)], axis=1)
    k = _rope(qkv[:, 512:640], cos, sin)
    v = qkv[:, 640:768]
    for t in range(steps):
        rows = slice(t * nb, (t + 1) * nb)
        q_ref[:, t * 512:(t + 1) * 512] = q[rows]
        k_ref[:, t * LANES:(t + 1) * LANES] = k[rows]
        v_ref[:, t * LANES:(t + 1) * LANES] = v[rows]

    glu = jnp.dot(xb, w_in_ref[:, 768:1792], preferred_element_type=F32)
    u = glu[:, 0:512] * _sigmoid(glu[:, 512:1024])
    c_conv = u.shape[1]
    n_taps = w_dw_ref.shape[0]
    n_hist = n_taps - 1
    seq = [st_ref[:, r * c_conv:(r + 1) * c_conv] for r in range(n_hist)]
    seq += [u[t * nb:(t + 1) * nb] for t in range(steps)]
    outs = []
    for t in range(steps):
        acc = jnp.broadcast_to(b_dw_ref[...], (nb, c_conv))
        for j in range(n_taps):
            acc = acc + seq[t + j] * w_dw_ref[j:j + 1, :]
        outs.append(acc)
    for r in range(n_hist):
        cst_ref[:, r * c_conv:(r + 1) * c_conv] = seq[steps + r]
    cn = _layer_norm(jnp.concatenate(outs, axis=0), cg_ref[...], cb_ref[...])
    cproj = _mm(cn * _sigmoid(cn), w_co_ref[...])

    gates = jnp.dot(xb, w_in_ref[:, 1792:3840], preferred_element_type=F32)
    ga_ref[...] = _sigmoid(gates[:, 0:1024])
    ct_ref[...] = _sigmoid(gates[:, 1024:2048]) * cproj


def _sample_attn_kernel(q_ref, kn_ref, vn_ref, ck_ref, cv_ref, sink_ref, o_ref, kwin_ref, vwin_ref, *, steps):
    n_seq = ck_ref.shape[0]
    wc = ck_ref.shape[1]
    half8 = _lane_half((steps, LANES))
    n_rows = N_HEADS * steps
    n_keys = 2 * wc
    row_t = lax.broadcasted_iota(jnp.int32, (n_rows, n_keys), 0) % steps
    col = lax.broadcasted_iota(jnp.int32, (n_rows, n_keys), 1)
    valid = jnp.where(col < wc, col - row_t - (wc - WINDOW) - 1, row_t - (col - wc)) >= 0
    sink = sink_ref[:, 0:1]
    pad = jnp.zeros((n_keys - wc - steps, LANES), F32)

    def one(s, carry):
        rows = pl.ds(pl.multiple_of(s * steps, steps), steps)
        qs = q_ref[rows, :]
        blocks = []
        for c in range(4):
            qc = qs[:, c * LANES:(c + 1) * LANES]
            qc_sw = _swap_halves(qc)
            h = c // 2
            for a in range(2):
                blocks.append(jnp.where(half8 == h, qc if a == h else qc_sw, 0.0))
        lhs = _bf(jnp.concatenate(blocks, axis=0))
        kc, kn = ck_ref[s], kn_ref[rows, :]
        vc, vn = cv_ref[s], vn_ref[rows, :]
        k_all = jnp.concatenate([kc, kn, pad], axis=0)
        v_all = jnp.concatenate([vc, vn, pad], axis=0)
        sc = jnp.where(valid, _mm_t(lhs, _bf(k_all)), NEG)
        e, r = _softmax_with_sink(sc, sink)
        out = jnp.dot(_bf(e), _bf(v_all), preferred_element_type=F32) * r
        chunks = []
        for c in range(4):
            h = c // 2
            lo = out[(2 * c) * steps:(2 * c + 1) * steps]
            hi = out[(2 * c + 1) * steps:(2 * c + 2) * steps]
            lo = lo if h == 0 else _swap_halves(lo)
            hi = hi if h == 1 else _swap_halves(hi)
            chunks.append(jnp.where(half8 == 0, lo, hi))
        o_ref[rows, :] = jnp.concatenate(chunks, axis=1)
        kwin_ref[s] = k_all[steps:steps + wc]
        vwin_ref[s] = v_all[steps:steps + wc]
        return carry

    lax.fori_loop(0, n_seq, one, 0)


def _sample_ffn_kernel(o_ref, ga_ref, ct_ref, x_ref, p_ref, st_ref, w_att_ref, w_out_ref, g1_ref, b1_ref,
                       w_up_ref, w_fc_ref, b_fc_ref, w_dn_ref, g2_ref, b2_ref, wpg_ref, wpp_ref, g3_ref, b3_ref,
                       y_ref, fst_ref, *, steps, alpha):
    nb = x_ref.shape[0]
    d_model = x_ref.shape[1] // steps
    d_ff = w_dn_ref.shape[0]
    n_taps = w_fc_ref.shape[0]
    n_hist = n_taps - 1
    width = 2 * d_ff

    o = _rows_time_major(o_ref, o_ref.shape[1] // steps, steps)
    x = _rows_time_major(x_ref, d_model, steps)
    p = _rows_time_major(p_ref, p_ref.shape[1] // steps, steps)
    att = _mm(o, w_att_ref[...])
    mixed = _mm(ga_ref[...] * att + ct_ref[...], w_out_ref[...])
    x1 = _layer_norm(alpha * x + mixed, g1_ref[...], b1_ref[...])

    xb = _bf(x1)
    f = jnp.zeros((steps * nb, d_model), F32)
    for c0 in range(0, d_ff, FF_CHUNK):
        halves = []
        for base in (c0, d_ff + c0):
            cols = slice(base, base + FF_CHUNK)
            up = jnp.dot(xb, w_up_ref[:, cols], preferred_element_type=F32)
            seq = [st_ref[:, r * width + base:r * width + base + FF_CHUNK] for r in range(n_hist)]
            seq += [up[t * nb:(t + 1) * nb] for t in range(steps)]
            outs = []
            for t in range(steps):
                acc = jnp.broadcast_to(b_fc_ref[:, cols], (nb, FF_CHUNK))
                for j in range(n_taps):
                    acc = acc + seq[t + j] * w_fc_ref[j:j + 1, cols]
                outs.append(acc)
            for r in range(n_hist):
                fst_ref[:, r * width + base:r * width + base + FF_CHUNK] = seq[steps + r]
            halves.append(jnp.concatenate(outs, axis=0))
        act = _gelu(halves[0]) * halves[1]
        f = f + jnp.dot(_bf(act), w_dn_ref[c0:c0 + FF_CHUNK, :], preferred_element_type=F32)

    y = _ffn_tail(x1, f, p, g2_ref[...], b2_ref[...], wpg_ref[...], wpp_ref[...], g3_ref[...], b3_ref[...], alpha)
    for t in range(steps):
        y_ref[:, t * d_model:(t + 1) * d_model] = y[t * nb:(t + 1) * nb]


def _const_spec(shape):
    zeros = (0,) * len(shape)
    return pl.BlockSpec(shape, lambda i: zeros, pipeline_mode=pl.Buffered(1))


def _const_out_spec(shape):
    zeros = (0,) * len(shape)
    return pl.BlockSpec(shape, lambda i: zeros)


def _row_spec(rows, width):
    return pl.BlockSpec((rows, width), lambda i: (i, 0))


def _params():
    return pltpu.CompilerParams(dimension_semantics=("arbitrary",), vmem_limit_bytes=VMEM_LIMIT)


def _rope_tables(pos):
    half = HEAD_DIM // 2
    inv = ROPE_THETA ** (-jnp.arange(half, dtype=F32) / half)
    ang = pos.astype(F32)[:, None] * inv[None, :]
    cos, sin = jnp.cos(ang), jnp.sin(ang)
    reps = LANES // HEAD_DIM
    return jnp.tile(jnp.concatenate([cos, cos], axis=1), (1, reps)), jnp.tile(jnp.concatenate([-sin, sin], axis=1), (1, reps))


def _row2d(v):
    return v.reshape(1, -1)


def _prompt_layer(x, p, cos, sin, w, alpha):
    t_len, d_model = x.shape
    n_taps, c_conv = w["w_dw"].shape
    d_ff = w["w_down"].shape[0]
    f_taps = w["w_fconv"].shape[0]
    conv_tail = -(-(n_taps - 1) // 8) * 8
    ffn_tail = -(-(f_taps - 1) // 8) * 8
    grid = (t_len // TM,)

    mixer_in = [w["sinks"], x, cos, sin, w["w_in"], w["w_attn_out"], w["w_dw"], _row2d(w["b_dw"]),
                _row2d(w["conv_ln_g"]), _row2d(w["conv_ln_b"]), w["w_conv_out"], w["w_out"],
                _row2d(w["ln1_g"]), _row2d(w["ln1_b"])]
    mixer_specs = [pl.BlockSpec(memory_space=pltpu.SMEM), _row_spec(TM, d_model), _row_spec(TM, LANES), _row_spec(TM, LANES)]
    mixer_specs += [_const_spec(a.shape) for a in mixer_in[4:]]
    x1, kwin, vwin, cst = pl.pallas_call(
        functools.partial(_prompt_mixer_kernel, alpha=alpha),
        grid=grid,
        in_specs=mixer_specs,
        out_specs=[_row_spec(TM, d_model), _const_out_spec((WINDOW, LANES)), _const_out_spec((WINDOW, LANES)),
                   _const_out_spec((n_taps - 1, c_conv))],
        out_shape=[jax.ShapeDtypeStruct((t_len, d_model), F32), jax.ShapeDtypeStruct((WINDOW, LANES), F32),
                   jax.ShapeDtypeStruct((WINDOW, LANES), F32), jax.ShapeDtypeStruct((n_taps - 1, c_conv), F32)],
        scratch_shapes=[pltpu.VMEM((WINDOW, LANES), BF16), pltpu.VMEM((WINDOW, LANES), BF16),
                        pltpu.VMEM((WINDOW, 2 * LANES), BF16),
                        pltpu.VMEM((c_conv // LANES, TM + conv_tail, LANES), F32),
                        pltpu.VMEM((TM, c_conv), F32)],
        compiler_params=_params(),
        name="prompt_mixer",
    )(*mixer_in)

    ffn_in = [x1, p, w["w_up"], w["w_fconv"], _row2d(w["b_fconv"]), w["w_down"], _row2d(w["ln2_g"]), _row2d(w["ln2_b"]),
              w["w_ple_gate"], w["w_ple_proj"], _row2d(w["ln3_g"]), _row2d(w["ln3_b"])]
    ffn_specs = [_row_spec(TM, d_model), _row_spec(TM, p.shape[1])] + [_const_spec(a.shape) for a in ffn_in[2:]]
    y, fst = pl.pallas_call(
        functools.partial(_prompt_ffn_kernel, alpha=alpha),
        grid=grid,
        in_specs=ffn_specs,
        out_specs=[_row_spec(TM, d_model), _const_out_spec((f_taps - 1, 2 * d_ff))],
        out_shape=[jax.ShapeDtypeStruct((t_len, d_model), F32), jax.ShapeDtypeStruct((f_taps - 1, 2 * d_ff), F32)],
        scratch_shapes=[pltpu.VMEM((2 * d_ff // LANES, TM + ffn_tail, LANES), F32)],
        compiler_params=_params(),
        name="prompt_ffn",
    )(*ffn_in)
    return y, kwin, vwin, cst, fst


def _sample_layer(x, p, cache_k, cache_v, st_conv, st_ffn, cos, sin, w, alpha):
    n_seq, wc = cache_k.shape[0], cache_k.shape[1]
    d_model = w["w_in"].shape[0]
    steps = x.shape[1] // d_model
    n_taps, c_conv = w["w_dw"].shape
    d_ff = w["w_down"].shape[0]
    att_w = N_HEADS * HEAD_DIM
    n_tok = n_seq * steps
    blk_rows = SEQ_BLK * steps
    grid = (n_seq // SEQ_BLK,)
    seq_spec = lambda width: pl.BlockSpec((SEQ_BLK, width), lambda i: (i, 0))

    proj_in = [x, cos, sin, st_conv, w["w_in"], w["w_dw"], _row2d(w["b_dw"]), _row2d(w["conv_ln_g"]),
               _row2d(w["conv_ln_b"]), w["w_conv_out"]]
    proj_specs = [seq_spec(x.shape[1]), _const_spec(cos.shape), _const_spec(sin.shape), seq_spec(st_conv.shape[1])]
    proj_specs += [_const_spec(a.shape) for a in proj_in[4:]]
    q, kn, vn, cst, ga, ct = pl.pallas_call(
        functools.partial(_sample_proj_kernel, steps=steps),
        grid=grid,
        in_specs=proj_specs,
        out_specs=[seq_spec(steps * att_w), seq_spec(steps * LANES), seq_spec(steps * LANES), seq_spec(st_conv.shape[1]),
                   _row_spec(blk_rows, d_model), _row_spec(blk_rows, d_model)],
        out_shape=[jax.ShapeDtypeStruct((n_seq, steps * att_w), F32), jax.ShapeDtypeStruct((n_seq, steps * LANES), F32),
                   jax.ShapeDtypeStruct((n_seq, steps * LANES), F32), jax.ShapeDtypeStruct(st_conv.shape, F32),
                   jax.ShapeDtypeStruct((n_tok, d_model), F32), jax.ShapeDtypeStruct((n_tok, d_model), F32)],
        compiler_params=_params(),
        name="sample_proj",
    )(*proj_in)

    sink_rows = jnp.broadcast_to(jnp.repeat(w["sinks"], steps)[:, None], (N_HEADS * steps, LANES))
    att_rows = ATT_BLK * steps
    cache_spec = pl.BlockSpec((ATT_BLK, wc, LANES), lambda i: (i, 0, 0))
    o, kwin, vwin = pl.pallas_call(
        functools.partial(_sample_attn_kernel, steps=steps),
        grid=(n_seq // ATT_BLK,),
        in_specs=[_row_spec(att_rows, att_w), _row_spec(att_rows, LANES), _row_spec(att_rows, LANES),
                  cache_spec, cache_spec, _const_spec(sink_rows.shape)],
        out_specs=[_row_spec(att_rows, att_w), cache_spec, cache_spec],
        out_shape=[jax.ShapeDtypeStruct((n_tok, att_w), F32), jax.ShapeDtypeStruct(cache_k.shape, F32),
                   jax.ShapeDtypeStruct(cache_v.shape, F32)],
        compiler_params=_params(),
        name="sample_attn",
    )(q.reshape(n_tok, att_w), kn.reshape(n_tok, LANES), vn.reshape(n_tok, LANES), cache_k, cache_v, sink_rows)

    ffn_in = [o.reshape(n_seq, steps * att_w), ga, ct, x, p, st_ffn, w["w_attn_out"], w["w_out"], _row2d(w["ln1_g"]),
              _row2d(w["ln1_b"]), w["w_up"], w["w_fconv"], _row2d(w["b_fconv"]), w["w_down"], _row2d(w["ln2_g"]),
              _row2d(w["ln2_b"]), w["w_ple_gate"], w["w_ple_proj"], _row2d(w["ln3_g"]), _row2d(w["ln3_b"])]
    ffn_specs = [seq_spec(steps * att_w), _row_spec(blk_rows, d_model), _row_spec(blk_rows, d_model), seq_spec(x.shape[1]),
                 seq_spec(p.shape[1]), seq_spec(st_ffn.shape[1])] + [_const_spec(a.shape) for a in ffn_in[6:]]
    y, fst = pl.pallas_call(
        functools.partial(_sample_ffn_kernel, steps=steps, alpha=alpha),
        grid=grid,
        in_specs=ffn_specs,
        out_specs=[seq_spec(x.shape[1]), seq_spec(st_ffn.shape[1])],
        out_shape=[jax.ShapeDtypeStruct(x.shape, F32), jax.ShapeDtypeStruct(st_ffn.shape, F32)],
        compiler_params=_params(),
        name="sample_ffn",
    )(*ffn_in)
    return y, kwin, vwin, cst, fst


def kernel(x_prompt, x_sample, cache_k, cache_v, state_conv, state_ffn_conv, p_prompt, p_sample, w_in, sinks, w_attn_out, w_dw, b_dw, conv_ln_g, conv_ln_b, w_conv_out, w_out, ln1_g, ln1_b, w_up, w_fconv, b_fconv, w_down, ln2_g, ln2_b, w_ple_gate, w_ple_proj, ln3_g, ln3_b):
    depth = w_in.shape[0]
    bp, t_len, d_model = x_prompt.shape
    n_seq, steps, _ = x_sample.shape
    wc = cache_k.shape[2]
    past_len = PAST_LEN
    assert bp == 1 and t_len % TM == 0 and n_seq % SEQ_BLK == 0 and n_seq % ATT_BLK == 0
    assert N_KV_HEADS * HEAD_DIM == LANES and wc == WINDOW and steps == 8
    alpha = (2 * depth) ** 0.25

    cos_p, sin_p = _rope_tables(jnp.arange(t_len))
    cos_s, sin_s = _rope_tables(past_len + jnp.arange(steps))
    cos_s = jnp.repeat(cos_s, SEQ_BLK, axis=0)
    sin_s = jnp.repeat(sin_s, SEQ_BLK, axis=0)

    matmul_weights = dict(w_in=w_in, w_attn_out=w_attn_out, w_conv_out=w_conv_out, w_out=w_out, w_up=w_up,
                          w_down=w_down, w_ple_gate=w_ple_gate, w_ple_proj=w_ple_proj)
    other = dict(sinks=sinks, w_dw=w_dw, b_dw=b_dw, conv_ln_g=conv_ln_g, conv_ln_b=conv_ln_b, ln1_g=ln1_g, ln1_b=ln1_b,
                 w_fconv=w_fconv, b_fconv=b_fconv, ln2_g=ln2_g, ln2_b=ln2_b, ln3_g=ln3_g, ln3_b=ln3_b)

    yp = x_prompt.reshape(t_len, d_model)
    ys = x_sample.reshape(n_seq, steps * d_model)
    outs = [[] for _ in range(8)]
    for l in range(depth):
        w = {name: _bf(a[l]) for name, a in matmul_weights.items()}
        w.update({name: a[l] for name, a in other.items()})
        yp, kp, vp, cp, fp = _prompt_layer(yp, p_prompt[l, 0], cos_p, sin_p, w, alpha)
        ys, ks, vs, cs, fs = _sample_layer(
            ys, p_sample[l].reshape(n_seq, -1), cache_k[l].reshape(n_seq, wc, LANES), cache_v[l].reshape(n_seq, wc, LANES),
            state_conv[l].reshape(n_seq, -1), state_ffn_conv[l].reshape(n_seq, -1), cos_s, sin_s, w, alpha)
        kv_shape = (N_KV_HEADS, HEAD_DIM)
        for lst, a in zip(outs, (kp.reshape(1, wc, *kv_shape), vp.reshape(1, wc, *kv_shape), cp[None], fp[None],
                                 ks.reshape(n_seq, wc, *kv_shape), vs.reshape(n_seq, wc, *kv_shape),
                                 cs.reshape(n_seq, *state_conv.shape[2:]), fs.reshape(n_seq, *state_ffn_conv.shape[2:]))):
            lst.append(a)
    return (yp.reshape(x_prompt.shape), ys.reshape(x_sample.shape)) + tuple(jnp.stack(lst) for lst in outs)
```

```python
import functools
import math

import jax
import jax.numpy as jnp
from jax import lax
from jax.experimental import pallas as pl
from jax.experimental.pallas import tpu as pltpu

F32 = jnp.float32
BF16 = jnp.bfloat16

LANES = 128
HEAD_DIM = 64
N_HEADS = 8
N_KV_HEADS = 2
ATT_WIDTH = N_HEADS * HEAD_DIM
KV_WIDTH = N_KV_HEADS * HEAD_DIM
WINDOW = 128
ROPE_THETA = 10000.0
PAST_LEN = 16384
LN_EPS = 1e-5
NEG = -1e30
GELU_C = math.sqrt(2.0 / math.pi)

TM = 512
SEQ_BLK = 32
ATT_BLK = 16
ATT_UNROLL = 4
FF_CHUNK = 256
CONV_ROWS = 128
VMEM_LIMIT = 56 * 1024 * 1024


def _in_proj_columns(d_model, c_conv):
    qkv = ATT_WIDTH + 2 * KV_WIDTH
    glu = qkv + 2 * c_conv
    return (0, qkv), (qkv, glu), (glu, glu + d_model), (glu + d_model, glu + 2 * d_model)


def _bf(x):
    return x.astype(BF16)


def _mm(a, w):
    return jnp.dot(_bf(a), w, preferred_element_type=F32)


def _mm_t(a, b):
    return lax.dot_general(a, b, (((1,), (1,)), ((), ())), preferred_element_type=F32)


def _sigmoid(x):
    return 0.5 * (jnp.tanh(0.5 * x) + 1.0)


def _gelu(x):
    return 0.5 * x * (1.0 + jnp.tanh(GELU_C * (x + 0.044715 * (x * x * x))))


def _layer_norm(x, g, b):
    mu = jnp.mean(x, axis=-1, keepdims=True)
    d = x - mu
    var = jnp.mean(d * d, axis=-1, keepdims=True)
    return d * lax.rsqrt(var + LN_EPS) * g + b


def _lane_half(shape):
    return (lax.broadcasted_iota(jnp.int32, shape, len(shape) - 1) // HEAD_DIM) % 2


def _first_half_of_head(shape):
    lane = lax.broadcasted_iota(jnp.int32, shape, len(shape) - 1)
    return (lane % HEAD_DIM) < (HEAD_DIM // 2)


def _rope(x, cos, sin_signed):
    partner = jnp.where(_first_half_of_head(x.shape), pltpu.roll(x, LANES - HEAD_DIM // 2, 1),
                        pltpu.roll(x, HEAD_DIM // 2, 1))
    return x * cos + partner * sin_signed


def _swap_halves(x):
    return pltpu.roll(x, HEAD_DIM, 1)


def _softmax_with_sink(s, sink):
    m = jnp.maximum(jnp.max(s, axis=-1, keepdims=True), sink)
    e = jnp.exp(s - m)
    den = jnp.sum(e, axis=-1, keepdims=True) + jnp.exp(sink - m)
    return e, 1.0 / den


def _ffn_tail_rows(x1, f, e_proj, ln2_g, ln2_b, wpg, ln3_g, ln3_b, alpha, row_groups):
    x2 = []
    gate = []
    for rows in row_groups:
        x2.append(_layer_norm(alpha * x1[rows] + f[rows], ln2_g, ln2_b))
        gate.append(_mm(x2[-1], wpg))
    return [_layer_norm(alpha * x2[n] + _sigmoid(gate[n]) * e_proj[rows], ln3_g, ln3_b)
            for n, rows in enumerate(row_groups)]


def _prompt_mixer_kernel(sinks_ref, rope_ref, x_ref, w_in_ref, w_att_ref, w_dw_ref, b_dw_ref,
                         cg_ref, cb_ref, w_co_ref, w_out_ref, g1_ref, b1_ref,
                         x1_ref, kwin_ref, vwin_ref, cst_ref,
                         kprev, kprev_sw, vprev2, useq, cbuf, rtab, *, alpha):
    i = pl.program_id(0)
    tail = useq.shape[1] - TM
    n_grp = useq.shape[0]
    n_taps = w_dw_ref.shape[0]
    first = tail - (n_taps - 1)
    d_model = x_ref.shape[1]
    c_conv = n_grp * LANES
    qkv_cols, glu_cols, ga_cols, gc_cols = _in_proj_columns(d_model, c_conv)
    inv_freq = rope_ref[0:1, :]
    n_blk = TM // WINDOW

    @pl.when(i == 0)
    def _():
        kprev[...] = jnp.zeros_like(kprev)
        kprev_sw[...] = jnp.zeros_like(kprev_sw)
        vprev2[...] = jnp.zeros_like(vprev2)
        useq[:, TM:TM + tail, :] = jnp.zeros((n_grp, tail, LANES), F32)
        ang = lax.broadcasted_iota(jnp.int32, (TM, LANES), 0).astype(F32) * inv_freq
        rtab[0] = jnp.cos(ang)
        rtab[1] = jnp.sin(ang)

    useq[:, 0:tail, :] = useq[:, TM:TM + tail, :]

    x = x_ref[...]
    xb = _bf(x)

    ang_b = (i * TM).astype(F32) * inv_freq
    cos_b, sin_b = jnp.cos(ang_b), jnp.sin(ang_b)
    cos_r, sin_r = rtab[0], rtab[1]
    cos = cos_b * cos_r - sin_b * sin_r
    sin = sin_b * cos_r + cos_b * sin_r
    sin = jnp.where(_first_half_of_head((TM, LANES)), -sin, sin)

    glu = jnp.dot(xb, w_in_ref[:, glu_cols[0]:glu_cols[1]], preferred_element_type=F32)
    qkv = jnp.dot(xb, w_in_ref[:, qkv_cols[0]:qkv_cols[1]], preferred_element_type=F32)

    u = glu[:, 0:c_conv] * _sigmoid(glu[:, c_conv:2 * c_conv])
    for c in range(n_grp):
        useq[c, tail:tail + TM, :] = u[:, c * LANES:(c + 1) * LANES]

    def conv_rows(r0):
        for c in range(n_grp):
            cols = slice(c * LANES, (c + 1) * LANES)
            acc = jnp.broadcast_to(b_dw_ref[:, cols], (CONV_ROWS, LANES))
            for j in range(n_taps):
                acc = acc + useq[c, pl.ds(first + j + r0, CONV_ROWS), :] * w_dw_ref[j:j + 1, cols]
            cbuf[r0:r0 + CONV_ROWS, cols] = acc

    gate_att = jnp.dot(xb, w_in_ref[:, ga_cols[0]:ga_cols[1]], preferred_element_type=F32)

    scale = HEAD_DIM ** -0.5
    q = [_rope(qkv[:, c * LANES:(c + 1) * LANES], cos, sin) * scale for c in range(ATT_WIDTH // LANES)]
    k = _rope(qkv[:, ATT_WIDTH:ATT_WIDTH + KV_WIDTH], cos, sin)
    v = qkv[:, ATT_WIDTH + KV_WIDTH:ATT_WIDTH + 2 * KV_WIDTH]
    kb, kb_sw = _bf(k), _bf(_swap_halves(k))
    vb2 = jnp.concatenate([_bf(v), _bf(_swap_halves(v))], axis=1)

    half_q = _lane_half((WINDOW, LANES))
    row = lax.broadcasted_iota(jnp.int32, (WINDOW, 2 * WINDOW), 0)
    col = lax.broadcasted_iota(jnp.int32, (WINDOW, 2 * WINDOW), 1)
    stack_same = [(c, a) for c in range(4) for a in range(2) if a == c // 2]
    stack_swap = [(c, a) for c in range(4) for a in range(2) if a != c // 2]
    gc_step = (gc_cols[1] - gc_cols[0]) // n_blk

    o_blocks = []
    gate_conv = []
    for bi in range(n_blk):
        r0 = bi * WINDOW
        if bi == 0:
            kp, kp_sw, vp2 = kprev[...], kprev_sw[...], vprev2[...]
            shift = jnp.where(i == 0, 2 * WINDOW, 0)
        else:
            kp, kp_sw, vp2 = kb[r0 - WINDOW:r0], kb_sw[r0 - WINDOW:r0], vb2[r0 - WINDOW:r0]
            shift = 0
        valid = jnp.where(col < WINDOW, col - row - shift - 1, row - (col - WINDOW)) >= 0
        kk = jnp.concatenate([kp, kb[r0:r0 + WINDOW]], axis=0)
        kk_sw = jnp.concatenate([kp_sw, kb_sw[r0:r0 + WINDOW]], axis=0)
        vv2 = jnp.concatenate([vp2, vb2[r0:r0 + WINDOW]], axis=0)
        scores = []
        for heads, keys in ((stack_same, kk), (stack_swap, kk_sw)):
            qm = jnp.concatenate(
                [_bf(jnp.where(half_q == a, q[c][r0:r0 + WINDOW], 0.0)) for c, a in heads], axis=0)
            scores.append(_mm_t(qm, keys))
        conv_rows(r0)
        head_out = {}
        for heads, s_all in ((stack_same, scores[0]), (stack_swap, scores[1])):
            probs, inv = [], []
            for n, (c, a) in enumerate(heads):
                s = jnp.where(valid, s_all[n * WINDOW:(n + 1) * WINDOW], NEG)
                e, r = _softmax_with_sink(s, sinks_ref[2 * c + a])
                probs.append(_bf(e))
                inv.append(r)
            if heads is stack_same:
                lo_col = gc_cols[0] + bi * gc_step
                gate_conv.append(jnp.dot(xb, w_in_ref[:, lo_col:lo_col + gc_step], preferred_element_type=F32))
            pv = jnp.dot(jnp.concatenate(probs, axis=0), vv2, preferred_element_type=F32)
            for n, (c, a) in enumerate(heads):
                head_out[(c, a)] = pv[n * WINDOW:(n + 1) * WINDOW] * inv[n]
        chunks = []
        for c in range(4):
            h = c // 2
            lo = head_out[(c, 0)][:, (0 if h == 0 else LANES):(LANES if h == 0 else 2 * LANES)]
            hi = head_out[(c, 1)][:, (0 if h == 1 else LANES):(LANES if h == 1 else 2 * LANES)]
            chunks.append(jnp.where(half_q == 0, lo, hi))
        o_blocks.append(jnp.concatenate(chunks, axis=1))
    o = jnp.concatenate(o_blocks, axis=0)
    att = _mm(o, w_att_ref[...])

    kprev[...] = kb[TM - WINDOW:]
    kprev_sw[...] = kb_sw[TM - WINDOW:]
    vprev2[...] = vb2[TM - WINDOW:]
    kwin_ref[...] = k[TM - WINDOW:]
    vwin_ref[...] = v[TM - WINDOW:]
    for c in range(n_grp):
        cst_ref[:, c * LANES:(c + 1) * LANES] = useq[c, pl.ds(TM + first, n_taps - 1), :]

    cn = _layer_norm(cbuf[...], cg_ref[...], cb_ref[...])
    cproj = _mm(cn * _sigmoid(cn), w_co_ref[...])

    gate_conv = jnp.concatenate(gate_conv, axis=1)
    row_groups = [slice(0, TM // 2), slice(TM // 2, TM)]
    mixed = []
    for rows in row_groups:
        merged = _sigmoid(gate_att[rows]) * att[rows] + _sigmoid(gate_conv[rows]) * cproj[rows]
        mixed.append(_mm(merged, w_out_ref[...]))
    for n, rows in enumerate(row_groups):
        x1_ref[rows, :] = _layer_norm(alpha * x[rows] + mixed[n], g1_ref[...], b1_ref[...])


def _prompt_ffn_kernel(x1_ref, p_ref, w_up_ref, w_fc_ref, b_fc_ref, w_dn_ref, g2_ref, b2_ref,
                       wpg_ref, wpp_ref, g3_ref, b3_ref,
                       y_ref, fst_ref, upseq, *, alpha):
    i = pl.program_id(0)
    n_slab = upseq.shape[0]
    tail = upseq.shape[1] - TM
    d_ff = w_dn_ref.shape[0]
    n_taps = w_fc_ref.shape[0]

    @pl.when(i == 0)
    def _():
        upseq[:, TM:TM + tail, :] = jnp.zeros((n_slab, tail, LANES), F32)

    upseq[:, 0:tail, :] = upseq[:, TM:TM + tail, :]

    x1 = x1_ref[...]
    xb = _bf(x1)

    def up_proj(c0):
        return [jnp.dot(xb, w_up_ref[:, base:base + FF_CHUNK], preferred_element_type=F32) for base in (c0, d_ff + c0)]

    def conv_act(c0, ups):
        halves = []
        for base, up in zip((c0, d_ff + c0), ups):
            parts = []
            for s0 in range(0, FF_CHUNK, LANES):
                slab = (base + s0) // LANES
                cols = slice(base + s0, base + s0 + LANES)
                cur = up[:, s0:s0 + LANES]
                upseq[slab, tail:tail + TM, :] = cur
                hcv = b_fc_ref[:, cols] + cur * w_fc_ref[n_taps - 1:n_taps, cols]
                for j in range(n_taps - 1):
                    hcv = hcv + upseq[slab, pl.ds(tail - (n_taps - 1) + j, TM), :] * w_fc_ref[j:j + 1, cols]
                parts.append(hcv)
            halves.append(jnp.concatenate(parts, axis=1))
        return _bf(_gelu(halves[0]) * halves[1])

    starts = list(range(0, d_ff, FF_CHUNK))
    ups = up_proj(starts[0])
    f = None
    e_proj = None
    for n, c0 in enumerate(starts):
        if n + 1 < len(starts):
            nxt = up_proj(starts[n + 1])
        else:
            nxt = None
            e_proj = _mm(p_ref[...], wpp_ref[...])
        act = conv_act(c0, ups)
        part = jnp.dot(act, w_dn_ref[c0:c0 + FF_CHUNK, :], preferred_element_type=F32)
        f = part if f is None else f + part
        ups = nxt

    for s in range(n_slab):
        fst_ref[:, s * LANES:(s + 1) * LANES] = upseq[s, pl.ds(TM + tail - (n_taps - 1), n_taps - 1), :]

    row_groups = [slice(0, TM // 2), slice(TM // 2, TM)]
    ys = _ffn_tail_rows(x1, f, e_proj, g2_ref[...], b2_ref[...], wpg_ref[...], g3_ref[...], b3_ref[...], alpha, row_groups)
    for rows, y in zip(row_groups, ys):
        y_ref[rows, :] = y


def _rows_time_major(ref, width, steps):
    return jnp.concatenate([ref[:, t * width:(t + 1) * width] for t in range(steps)], axis=0)


def _sample_proj_kernel(rope_ref, x_ref, st_ref, w_in_ref, w_dw_ref, b_dw_ref, cg_ref, cb_ref, w_co_ref,
                        q_ref, k_ref, v_ref, cst_ref, ga_ref, ct_ref, *, steps):
    nb = x_ref.shape[0]
    d_model = x_ref.shape[1] // steps
    n_taps, c_conv = w_dw_ref.shape
    n_hist = n_taps - 1
    qkv_cols, glu_cols, ga_cols, gc_cols = _in_proj_columns(d_model, c_conv)
    x = _rows_time_major(x_ref, d_model, steps)
    xb = _bf(x)

    pos = (PAST_LEN + lax.broadcasted_iota(jnp.int32, (steps, LANES), 0)).astype(F32)
    ang = pos * rope_ref[0:1, :]
    cos8 = jnp.cos(ang)
    sin8 = jnp.where(_first_half_of_head((steps, LANES)), -jnp.sin(ang), jnp.sin(ang))
    cos = jnp.concatenate([jnp.broadcast_to(cos8[t:t + 1], (nb, LANES)) for t in range(steps)], axis=0)
    sin = jnp.concatenate([jnp.broadcast_to(sin8[t:t + 1], (nb, LANES)) for t in range(steps)], axis=0)

    qkv = jnp.dot(xb, w_in_ref[:, qkv_cols[0]:qkv_cols[1]], preferred_element_type=F32)
    scale = HEAD_DIM ** -0.5
    q = jnp.concatenate([_rope(qkv[:, c * LANES:(c + 1) * LANES], cos, sin) * scale
                         for c in range(ATT_WIDTH // LANES)], axis=1)
    k = _rope(qkv[:, ATT_WIDTH:ATT_WIDTH + KV_WIDTH], cos, sin)
    v = qkv[:, ATT_WIDTH + KV_WIDTH:ATT_WIDTH + 2 * KV_WIDTH]
    for t in range(steps):
        rows = slice(t * nb, (t + 1) * nb)
        q_ref[:, t * ATT_WIDTH:(t + 1) * ATT_WIDTH] = q[rows]
        k_ref[:, t * KV_WIDTH:(t + 1) * KV_WIDTH] = k[rows]
        v_ref[:, t * KV_WIDTH:(t + 1) * KV_WIDTH] = v[rows]

    glu = jnp.dot(xb, w_in_ref[:, glu_cols[0]:glu_cols[1]], preferred_element_type=F32)
    u = glu[:, 0:c_conv] * _sigmoid(glu[:, c_conv:2 * c_conv])
    seq = [st_ref[r] for r in range(n_hist)]
    seq += [u[t * nb:(t + 1) * nb] for t in range(steps)]
    outs = []
    for t in range(steps):
        acc = jnp.broadcast_to(b_dw_ref[...], (nb, c_conv))
        for j in range(n_taps):
            acc = acc + seq[t + j] * w_dw_ref[j:j + 1, :]
        outs.append(acc)
    for r in range(n_hist):
        cst_ref[r] = seq[steps + r]
    cn = _layer_norm(jnp.concatenate(outs, axis=0), cg_ref[...], cb_ref[...])
    cproj = _mm(cn * _sigmoid(cn), w_co_ref[...])

    gates = jnp.dot(xb, w_in_ref[:, ga_cols[0]:gc_cols[1]], preferred_element_type=F32)
    ga_ref[...] = _sigmoid(gates[:, 0:d_model])
    ct_ref[...] = _sigmoid(gates[:, d_model:2 * d_model]) * cproj


def _sample_attn_kernel(q_ref, kn_ref, vn_ref, ckt_ref, cvt_ref, sink_ref, o_ref, kwint_ref, vwint_ref, *, steps):
    n_seq = ckt_ref.shape[0]
    wc = ckt_ref.shape[2]
    half8 = _lane_half((steps, LANES))
    n_rows = N_HEADS * steps
    row_t = lax.broadcasted_iota(jnp.int32, (n_rows, 2 * wc), 0) % steps
    col = lax.broadcasted_iota(jnp.int32, (n_rows, 2 * wc), 1)
    new_t = col - (2 * wc - steps)
    valid = jnp.where(col < wc, col - row_t - (wc - WINDOW) - 1, jnp.minimum(new_t, row_t - new_t)) >= 0
    sink = sink_ref[:, 0:1]
    lane_w = lax.broadcasted_iota(jnp.int32, (LANES, wc), 1)
    keep_old = lane_w < (wc - steps)
    pad = jnp.zeros((wc - steps, LANES), F32)

    def one(s):
        rows = pl.ds(pl.multiple_of(s * steps, steps), steps)
        qs = q_ref[rows, :]
        blocks = []
        for c in range(ATT_WIDTH // LANES):
            qc = qs[:, c * LANES:(c + 1) * LANES]
            qc_sw = _swap_halves(qc)
            h = c // 2
            for a in range(2):
                blocks.append(jnp.where(half8 == h, qc if a == h else qc_sw, 0.0))
        lhs = _bf(jnp.concatenate(blocks, axis=0))
        kct, vct = ckt_ref[s], cvt_ref[s]
        knt = jnp.concatenate([pad, kn_ref[rows, :]], axis=0).T
        vnt = jnp.concatenate([pad, vn_ref[rows, :]], axis=0).T
        sc = jnp.dot(lhs, _bf(jnp.concatenate([kct, knt], axis=1)), preferred_element_type=F32)
        e, r = _softmax_with_sink(jnp.where(valid, sc, NEG), sink)
        out = _mm_t(_bf(e), _bf(jnp.concatenate([vct, vnt], axis=1))) * r
        chunks = []
        for c in range(ATT_WIDTH // LANES):
            h = c // 2
            lo = out[(2 * c) * steps:(2 * c + 1) * steps]
            hi = out[(2 * c + 1) * steps:(2 * c + 2) * steps]
            lo = lo if h == 0 else _swap_halves(lo)
            hi = hi if h == 1 else _swap_halves(hi)
            chunks.append(jnp.where(half8 == 0, lo, hi))
        o_ref[rows, :] = jnp.concatenate(chunks, axis=1)
        kwint_ref[s] = jnp.where(keep_old, pltpu.roll(kct, wc - steps, 1), knt)
        vwint_ref[s] = jnp.where(keep_old, pltpu.roll(vct, wc - steps, 1), vnt)

    def group(g, carry):
        for n in range(ATT_UNROLL):
            one(g * ATT_UNROLL + n)
        return carry

    lax.fori_loop(0, n_seq // ATT_UNROLL, group, 0)


def _sample_ffn_kernel(o_ref, ga_ref, ct_ref, x_ref, p_ref, st_ref, w_att_ref, w_out_ref, g1_ref, b1_ref,
                       w_up_ref, w_fc_ref, b_fc_ref, w_dn_ref, g2_ref, b2_ref, wpg_ref, wpp_ref, g3_ref, b3_ref,
                       y_ref, fst_ref, *, steps, alpha):
    nb = x_ref.shape[0]
    d_model = x_ref.shape[1] // steps
    d_ff = w_dn_ref.shape[0]
    n_taps = w_fc_ref.shape[0]
    n_hist = n_taps - 1
    width = 2 * d_ff

    o = _rows_time_major(o_ref, o_ref.shape[1] // steps, steps)
    x = _rows_time_major(x_ref, d_model, steps)
    p = _rows_time_major(p_ref, p_ref.shape[1] // steps, steps)
    att = _mm(o, w_att_ref[...])
    mixed = _mm(ga_ref[...] * att + ct_ref[...], w_out_ref[...])
    x1 = _layer_norm(alpha * x + mixed, g1_ref[...], b1_ref[...])

    xb = _bf(x1)
    f = jnp.zeros((steps * nb, d_model), F32)
    for c0 in range(0, d_ff, FF_CHUNK):
        halves = []
        for base in (c0, d_ff + c0):
            cols = slice(base, base + FF_CHUNK)
            up = jnp.dot(xb, w_up_ref[:, cols], preferred_element_type=F32)
            seq = [st_ref[:, r * width + base:r * width + base + FF_CHUNK] for r in range(n_hist)]
            seq += [up[t * nb:(t + 1) * nb] for t in range(steps)]
            outs = []
            for t in range(steps):
                acc = jnp.broadcast_to(b_fc_ref[:, cols], (nb, FF_CHUNK))
                for j in range(n_taps):
                    acc = acc + seq[t + j] * w_fc_ref[j:j + 1, cols]
                outs.append(acc)
            for r in range(n_hist):
                fst_ref[:, r * width + base:r * width + base + FF_CHUNK] = seq[steps + r]
            halves.append(jnp.concatenate(outs, axis=0))
        act = _gelu(halves[0]) * halves[1]
        f = f + jnp.dot(_bf(act), w_dn_ref[c0:c0 + FF_CHUNK, :], preferred_element_type=F32)

    e_proj = _mm(p, wpp_ref[...])
    y, = _ffn_tail_rows(x1, f, e_proj, g2_ref[...], b2_ref[...], wpg_ref[...], g3_ref[...], b3_ref[...], alpha,
                        [slice(0, steps * nb)])
    for t in range(steps):
        y_ref[:, t * d_model:(t + 1) * d_model] = y[t * nb:(t + 1) * nb]


def _const_spec(shape):
    zeros = (0,) * len(shape)
    return pl.BlockSpec(shape, lambda i: zeros, pipeline_mode=pl.Buffered(1))


def _const_out_spec(shape):
    zeros = (0,) * len(shape)
    return pl.BlockSpec(shape, lambda i: zeros)


def _row_spec(rows, width):
    return pl.BlockSpec((rows, width), lambda i: (i, 0))


def _params():
    return pltpu.CompilerParams(dimension_semantics=("arbitrary",), vmem_limit_bytes=VMEM_LIMIT)


def _rope_rows():
    half = HEAD_DIM // 2
    inv = ROPE_THETA ** (-jnp.arange(half, dtype=F32) / half)
    return jnp.broadcast_to(jnp.tile(inv, LANES // half)[None, :], (8, LANES))


def _row2d(v):
    return v.reshape(1, -1)


def _prompt_layer(x, p, rope_rows, w, alpha):
    t_len, d_model = x.shape
    n_taps, c_conv = w["w_dw"].shape
    d_ff = w["w_down"].shape[0]
    f_taps = w["w_fconv"].shape[0]
    conv_tail = -(-(n_taps - 1) // 8) * 8
    ffn_tail = -(-(f_taps - 1) // 8) * 8
    grid = (t_len // TM,)

    mixer_in = [w["sinks"], rope_rows, x, w["w_in"], w["w_attn_out"], w["w_dw"], _row2d(w["b_dw"]),
                _row2d(w["conv_ln_g"]), _row2d(w["conv_ln_b"]), w["w_conv_out"], w["w_out"],
                _row2d(w["ln1_g"]), _row2d(w["ln1_b"])]
    mixer_specs = [pl.BlockSpec(memory_space=pltpu.SMEM), _const_spec(rope_rows.shape), _row_spec(TM, d_model)]
    mixer_specs += [_const_spec(a.shape) for a in mixer_in[3:]]
    x1, kwin, vwin, cst = pl.pallas_call(
        functools.partial(_prompt_mixer_kernel, alpha=alpha),
        grid=grid,
        in_specs=mixer_specs,
        out_specs=[_row_spec(TM, d_model), _const_out_spec((WINDOW, LANES)), _const_out_spec((WINDOW, LANES)),
                   _const_out_spec((n_taps - 1, c_conv))],
        out_shape=[jax.ShapeDtypeStruct((t_len, d_model), F32), jax.ShapeDtypeStruct((WINDOW, LANES), F32),
                   jax.ShapeDtypeStruct((WINDOW, LANES), F32), jax.ShapeDtypeStruct((n_taps - 1, c_conv), F32)],
        scratch_shapes=[pltpu.VMEM((WINDOW, LANES), BF16), pltpu.VMEM((WINDOW, LANES), BF16),
                        pltpu.VMEM((WINDOW, 2 * LANES), BF16),
                        pltpu.VMEM((c_conv // LANES, TM + conv_tail, LANES), F32),
                        pltpu.VMEM((TM, c_conv), F32),
                        pltpu.VMEM((2, TM, LANES), F32)],
        compiler_params=_params(),
        name="prompt_mixer",
    )(*mixer_in)

    ffn_in = [x1, p, w["w_up"], w["w_fconv"], _row2d(w["b_fconv"]), w["w_down"], _row2d(w["ln2_g"]), _row2d(w["ln2_b"]),
              w["w_ple_gate"], w["w_ple_proj"], _row2d(w["ln3_g"]), _row2d(w["ln3_b"])]
    ffn_specs = [_row_spec(TM, d_model), _row_spec(TM, p.shape[1])] + [_const_spec(a.shape) for a in ffn_in[2:]]
    y, fst = pl.pallas_call(
        functools.partial(_prompt_ffn_kernel, alpha=alpha),
        grid=grid,
        in_specs=ffn_specs,
        out_specs=[_row_spec(TM, d_model), _const_out_spec((f_taps - 1, 2 * d_ff))],
        out_shape=[jax.ShapeDtypeStruct((t_len, d_model), F32), jax.ShapeDtypeStruct((f_taps - 1, 2 * d_ff), F32)],
        scratch_shapes=[pltpu.VMEM((2 * d_ff // LANES, TM + ffn_tail, LANES), F32)],
        compiler_params=_params(),
        name="prompt_ffn",
    )(*ffn_in)
    return y, kwin, vwin, cst, fst


def _sample_layer(x, p, cache_kt, cache_vt, st_conv, st_ffn, rope_rows, w, alpha):
    n_seq, wc = cache_kt.shape[0], cache_kt.shape[2]
    d_model = w["w_in"].shape[0]
    steps = x.shape[1] // d_model
    n_tok = n_seq * steps
    blk_rows = SEQ_BLK * steps
    grid = (n_seq // SEQ_BLK,)
    seq_spec = lambda width: pl.BlockSpec((SEQ_BLK, width), lambda i: (i, 0))
    hist_spec = pl.BlockSpec((st_conv.shape[0], SEQ_BLK, st_conv.shape[2]), lambda i: (0, i, 0))

    proj_in = [rope_rows, x, st_conv, w["w_in"], w["w_dw"], _row2d(w["b_dw"]), _row2d(w["conv_ln_g"]),
               _row2d(w["conv_ln_b"]), w["w_conv_out"]]
    proj_specs = [_const_spec(rope_rows.shape), seq_spec(x.shape[1]), hist_spec]
    proj_specs += [_const_spec(a.shape) for a in proj_in[3:]]
    q, kn, vn, cst, ga, ct = pl.pallas_call(
        functools.partial(_sample_proj_kernel, steps=steps),
        grid=grid,
        in_specs=proj_specs,
        out_specs=[seq_spec(steps * ATT_WIDTH), seq_spec(steps * KV_WIDTH), seq_spec(steps * KV_WIDTH), hist_spec,
                   _row_spec(blk_rows, d_model), _row_spec(blk_rows, d_model)],
        out_shape=[jax.ShapeDtypeStruct((n_seq, steps * ATT_WIDTH), F32), jax.ShapeDtypeStruct((n_seq, steps * KV_WIDTH), F32),
                   jax.ShapeDtypeStruct((n_seq, steps * KV_WIDTH), F32), jax.ShapeDtypeStruct(st_conv.shape, F32),
                   jax.ShapeDtypeStruct((n_tok, d_model), F32), jax.ShapeDtypeStruct((n_tok, d_model), F32)],
        compiler_params=_params(),
        name="sample_proj",
    )(*proj_in)

    sink_rows = jnp.broadcast_to(jnp.repeat(w["sinks"], steps)[:, None], (N_HEADS * steps, LANES))
    att_rows = ATT_BLK * steps
    cache_spec = pl.BlockSpec((ATT_BLK, KV_WIDTH, wc), lambda i: (i, 0, 0))
    o, kwin_t, vwin_t = pl.pallas_call(
        functools.partial(_sample_attn_kernel, steps=steps),
        grid=(n_seq // ATT_BLK,),
        in_specs=[_row_spec(att_rows, ATT_WIDTH), _row_spec(att_rows, KV_WIDTH), _row_spec(att_rows, KV_WIDTH),
                  cache_spec, cache_spec, _const_spec(sink_rows.shape)],
        out_specs=[_row_spec(att_rows, ATT_WIDTH), cache_spec, cache_spec],
        out_shape=[jax.ShapeDtypeStruct((n_tok, ATT_WIDTH), F32), jax.ShapeDtypeStruct(cache_kt.shape, F32),
                   jax.ShapeDtypeStruct(cache_vt.shape, F32)],
        compiler_params=_params(),
        name="sample_attn",
    )(q.reshape(n_tok, ATT_WIDTH), kn.reshape(n_tok, KV_WIDTH), vn.reshape(n_tok, KV_WIDTH), cache_kt, cache_vt, sink_rows)

    ffn_in = [o.reshape(n_seq, steps * ATT_WIDTH), ga, ct, x, p, st_ffn, w["w_attn_out"], w["w_out"], _row2d(w["ln1_g"]),
              _row2d(w["ln1_b"]), w["w_up"], w["w_fconv"], _row2d(w["b_fconv"]), w["w_down"], _row2d(w["ln2_g"]),
              _row2d(w["ln2_b"]), w["w_ple_gate"], w["w_ple_proj"], _row2d(w["ln3_g"]), _row2d(w["ln3_b"])]
    ffn_specs = [seq_spec(steps * ATT_WIDTH), _row_spec(blk_rows, d_model), _row_spec(blk_rows, d_model), seq_spec(x.shape[1]),
                 seq_spec(p.shape[1]), seq_spec(st_ffn.shape[1])] + [_const_spec(a.shape) for a in ffn_in[6:]]
    y, fst = pl.pallas_call(
        functools.partial(_sample_ffn_kernel, steps=steps, alpha=alpha),
        grid=grid,
        in_specs=ffn_specs,
        out_specs=[seq_spec(x.shape[1]), seq_spec(st_ffn.shape[1])],
        out_shape=[jax.ShapeDtypeStruct(x.shape, F32), jax.ShapeDtypeStruct(st_ffn.shape, F32)],
        compiler_params=_params(),
        name="sample_ffn",
    )(*ffn_in)
    return y, kwin_t, vwin_t, cst, fst


def kernel(x_prompt, x_sample, cache_k, cache_v, state_conv, state_ffn_conv, p_prompt, p_sample, w_in, sinks, w_attn_out, w_dw, b_dw, conv_ln_g, conv_ln_b, w_conv_out, w_out, ln1_g, ln1_b, w_up, w_fconv, b_fconv, w_down, ln2_g, ln2_b, w_ple_gate, w_ple_proj, ln3_g, ln3_b):
    depth = w_in.shape[0]
    bp, t_len, d_model = x_prompt.shape
    n_seq, steps, _ = x_sample.shape
    wc = cache_k.shape[2]
    assert bp == 1 and t_len % TM == 0 and n_seq % SEQ_BLK == 0 and n_seq % ATT_BLK == 0 and ATT_BLK % ATT_UNROLL == 0
    assert KV_WIDTH == LANES and wc == WINDOW and wc == LANES and steps == 8
    assert cache_k.shape[3:] == (N_KV_HEADS, HEAD_DIM)
    alpha = (2 * depth) ** 0.25
    rope_rows = _rope_rows()

    matmul_weights = dict(w_in=w_in, w_attn_out=w_attn_out, w_conv_out=w_conv_out, w_out=w_out, w_up=w_up,
                          w_down=w_down, w_ple_gate=w_ple_gate, w_ple_proj=w_ple_proj)
    other = dict(sinks=sinks, w_dw=w_dw, b_dw=b_dw, conv_ln_g=conv_ln_g, conv_ln_b=conv_ln_b, ln1_g=ln1_g, ln1_b=ln1_b,
                 w_fconv=w_fconv, b_fconv=b_fconv, ln2_g=ln2_g, ln2_b=ln2_b, ln3_g=ln3_g, ln3_b=ln3_b)

    def keys_on_lanes(c):
        return jnp.transpose(c, (0, 2, 3, 1)).reshape(n_seq, KV_WIDTH, wc)

    def keys_on_rows(ct):
        return jnp.transpose(ct.reshape(n_seq, N_KV_HEADS, HEAD_DIM, wc), (0, 3, 1, 2))

    yp = x_prompt.reshape(t_len, d_model)
    ys = x_sample.reshape(n_seq, steps * d_model)
    outs = [[] for _ in range(8)]
    for l in range(depth):
        w = {name: _bf(a[l]) for name, a in matmul_weights.items()}
        w.update({name: a[l] for name, a in other.items()})
        yp, kp, vp, cp, fp = _prompt_layer(yp, p_prompt[l, 0], rope_rows, w, alpha)
        ys, ks_t, vs_t, cs, fs = _sample_layer(
            ys, p_sample[l].reshape(n_seq, -1), keys_on_lanes(cache_k[l]), keys_on_lanes(cache_v[l]),
            jnp.transpose(state_conv[l], (1, 0, 2)), state_ffn_conv[l].reshape(n_seq, -1), rope_rows, w, alpha)
        kv_shape = (N_KV_HEADS, HEAD_DIM)
        for lst, a in zip(outs, (kp.reshape(1, wc, *kv_shape), vp.reshape(1, wc, *kv_shape), cp[None], fp[None],
                                 keys_on_rows(ks_t), keys_on_rows(vs_t), jnp.transpose(cs, (1, 0, 2)),
                                 fs.reshape(n_seq, *state_ffn_conv.shape[2:]))):
            lst.append(a)
    return (yp.reshape(x_prompt.shape), ys.reshape(x_sample.shape)) + tuple(jnp.stack(lst) for lst in outs)
```

```python
import functools
import math

import jax
import jax.numpy as jnp
from jax import lax
from jax.experimental import pallas as pl
from jax.experimental.pallas import tpu as pltpu

F32 = jnp.float32
BF16 = jnp.bfloat16

LANES = 128
HEAD_DIM = 64
N_HEADS = 8
N_KV_HEADS = 2
ATT_WIDTH = N_HEADS * HEAD_DIM
KV_WIDTH = N_KV_HEADS * HEAD_DIM
WINDOW = 128
ROPE_THETA = 10000.0
PAST_LEN = 16384
LN_EPS = 1e-5
NEG = -1e30
GELU_C = math.sqrt(2.0 / math.pi)

TM = 512
SEQ_BLK = 32
ATT_BLK = 16
ATT_UNROLL = 4
FF_CHUNK = 256
DOWN_GROUP = 4
CONV_ROWS = 128
VMEM_LIMIT = 56 * 1024 * 1024


def _in_proj_columns(d_model, c_conv):
    qkv = ATT_WIDTH + 2 * KV_WIDTH
    glu = qkv + 2 * c_conv
    return (0, qkv), (qkv, glu), (glu, glu + d_model), (glu + d_model, glu + 2 * d_model)


def _bf(x):
    return x.astype(BF16)


def _mm(a, w):
    return jnp.dot(_bf(a), w, preferred_element_type=F32)


def _mm_t(a, b):
    return lax.dot_general(a, b, (((1,), (1,)), ((), ())), preferred_element_type=F32)


def _sigmoid(x):
    return 0.5 * (jnp.tanh(0.5 * x) + 1.0)


def _gelu(x):
    return 0.5 * x * (1.0 + jnp.tanh(GELU_C * (x + 0.044715 * (x * x * x))))


def _layer_norm(x, g, b):
    mu = jnp.mean(x, axis=-1, keepdims=True)
    d = x - mu
    var = jnp.mean(d * d, axis=-1, keepdims=True)
    return d * lax.rsqrt(var + LN_EPS) * g + b


def _lane_half(shape):
    return (lax.broadcasted_iota(jnp.int32, shape, len(shape) - 1) // HEAD_DIM) % 2


def _first_half_of_head(shape):
    lane = lax.broadcasted_iota(jnp.int32, shape, len(shape) - 1)
    return (lane % HEAD_DIM) < (HEAD_DIM // 2)


def _rope(x, cos, sin_signed):
    partner = jnp.where(_first_half_of_head(x.shape), pltpu.roll(x, LANES - HEAD_DIM // 2, 1),
                        pltpu.roll(x, HEAD_DIM // 2, 1))
    return x * cos + partner * sin_signed


def _swap_halves(x):
    return pltpu.roll(x, HEAD_DIM, 1)


def _softmax_with_sink(s, sink):
    m = jnp.maximum(jnp.max(s, axis=-1, keepdims=True), sink)
    e = jnp.exp(s - m)
    den = jnp.sum(e, axis=-1, keepdims=True) + jnp.exp(sink - m)
    return e, 1.0 / den


def _ffn_tail_rows(x1, f, e_proj, ln2_g, ln2_b, wpg, ln3_g, ln3_b, alpha, row_groups):
    x2 = []
    gate = []
    for rows in row_groups:
        x2.append(_layer_norm(alpha * x1[rows] + f[rows], ln2_g, ln2_b))
        gate.append(_mm(x2[-1], wpg))
    return [_layer_norm(alpha * x2[n] + _sigmoid(gate[n]) * e_proj[rows], ln3_g, ln3_b)
            for n, rows in enumerate(row_groups)]


def _prompt_mixer_kernel(sinks_ref, rope_ref, x_ref, w_in_ref, w_att_ref, w_dw_ref, b_dw_ref,
                         cg_ref, cb_ref, w_co_ref, w_out_ref, g1_ref, b1_ref,
                         x1_ref, kwin_ref, vwin_ref, cst_ref,
                         kprev, kprev_sw, vprev2, useq, cbuf, rtab, *, alpha):
    i = pl.program_id(0)
    tail = useq.shape[1] - TM
    n_grp = useq.shape[0]
    n_taps = w_dw_ref.shape[0]
    first = tail - (n_taps - 1)
    d_model = x_ref.shape[1]
    c_conv = n_grp * LANES
    qkv_cols, glu_cols, ga_cols, gc_cols = _in_proj_columns(d_model, c_conv)
    inv_freq = rope_ref[0:1, :]
    n_blk = TM // WINDOW

    @pl.when(i == 0)
    def _():
        kprev[...] = jnp.zeros_like(kprev)
        kprev_sw[...] = jnp.zeros_like(kprev_sw)
        vprev2[...] = jnp.zeros_like(vprev2)
        useq[:, TM:TM + tail, :] = jnp.zeros((n_grp, tail, LANES), F32)
        ang = lax.broadcasted_iota(jnp.int32, (TM, LANES), 0).astype(F32) * inv_freq
        rtab[0] = jnp.cos(ang)
        rtab[1] = jnp.sin(ang)

    useq[:, 0:tail, :] = useq[:, TM:TM + tail, :]

    x = x_ref[...]
    xb = _bf(x)

    ang_b = (i * TM).astype(F32) * inv_freq
    cos_b, sin_b = jnp.cos(ang_b), jnp.sin(ang_b)
    cos_r, sin_r = rtab[0], rtab[1]
    cos = cos_b * cos_r - sin_b * sin_r
    sin = sin_b * cos_r + cos_b * sin_r
    sin = jnp.where(_first_half_of_head((TM, LANES)), -sin, sin)

    glu = jnp.dot(xb, w_in_ref[:, glu_cols[0]:glu_cols[1]], preferred_element_type=F32)
    qkv = jnp.dot(xb, w_in_ref[:, qkv_cols[0]:qkv_cols[1]], preferred_element_type=F32)

    u = glu[:, 0:c_conv] * _sigmoid(glu[:, c_conv:2 * c_conv])
    for c in range(n_grp):
        useq[c, tail:tail + TM, :] = u[:, c * LANES:(c + 1) * LANES]

    def conv_rows(r0):
        for c in range(n_grp):
            cols = slice(c * LANES, (c + 1) * LANES)
            acc = jnp.broadcast_to(b_dw_ref[:, cols], (CONV_ROWS, LANES))
            for j in range(n_taps):
                acc = acc + useq[c, pl.ds(first + j + r0, CONV_ROWS), :] * w_dw_ref[j:j + 1, cols]
            cbuf[r0:r0 + CONV_ROWS, cols] = acc

    gate_att = jnp.dot(xb, w_in_ref[:, ga_cols[0]:ga_cols[1]], preferred_element_type=F32)

    scale = HEAD_DIM ** -0.5
    q = [_rope(qkv[:, c * LANES:(c + 1) * LANES], cos, sin) * scale for c in range(ATT_WIDTH // LANES)]
    k = _rope(qkv[:, ATT_WIDTH:ATT_WIDTH + KV_WIDTH], cos, sin)
    v = qkv[:, ATT_WIDTH + KV_WIDTH:ATT_WIDTH + 2 * KV_WIDTH]
    kb, kb_sw = _bf(k), _bf(_swap_halves(k))
    vb2 = jnp.concatenate([_bf(v), _bf(_swap_halves(v))], axis=1)

    half_q = _lane_half((WINDOW, LANES))
    row = lax.broadcasted_iota(jnp.int32, (WINDOW, 2 * WINDOW), 0)
    col = lax.broadcasted_iota(jnp.int32, (WINDOW, 2 * WINDOW), 1)
    stack_same = [(c, a) for c in range(4) for a in range(2) if a == c // 2]
    stack_swap = [(c, a) for c in range(4) for a in range(2) if a != c // 2]
    gc_step = (gc_cols[1] - gc_cols[0]) // n_blk

    o_blocks = []
    gate_conv = []
    for bi in range(n_blk):
        r0 = bi * WINDOW
        if bi == 0:
            kp, kp_sw, vp2 = kprev[...], kprev_sw[...], vprev2[...]
            shift = jnp.where(i == 0, 2 * WINDOW, 0)
        else:
            kp, kp_sw, vp2 = kb[r0 - WINDOW:r0], kb_sw[r0 - WINDOW:r0], vb2[r0 - WINDOW:r0]
            shift = 0
        valid = jnp.where(col < WINDOW, col - row - shift - 1, row - (col - WINDOW)) >= 0
        kk = jnp.concatenate([kp, kb[r0:r0 + WINDOW]], axis=0)
        kk_sw = jnp.concatenate([kp_sw, kb_sw[r0:r0 + WINDOW]], axis=0)
        vv2 = jnp.concatenate([vp2, vb2[r0:r0 + WINDOW]], axis=0)
        scores = []
        for heads, keys in ((stack_same, kk), (stack_swap, kk_sw)):
            qm = jnp.concatenate(
                [_bf(jnp.where(half_q == a, q[c][r0:r0 + WINDOW], 0.0)) for c, a in heads], axis=0)
            scores.append(_mm_t(qm, keys))
        conv_rows(r0)
        head_out = {}
        for heads, s_all in ((stack_same, scores[0]), (stack_swap, scores[1])):
            probs, inv = [], []
            for n, (c, a) in enumerate(heads):
                s = jnp.where(valid, s_all[n * WINDOW:(n + 1) * WINDOW], NEG)
                e, r = _softmax_with_sink(s, sinks_ref[2 * c + a])
                probs.append(_bf(e))
                inv.append(r)
            if heads is stack_same:
                lo_col = gc_cols[0] + bi * gc_step
                gate_conv.append(jnp.dot(xb, w_in_ref[:, lo_col:lo_col + gc_step], preferred_element_type=F32))
            pv = jnp.dot(jnp.concatenate(probs, axis=0), vv2, preferred_element_type=F32)
            for n, (c, a) in enumerate(heads):
                head_out[(c, a)] = pv[n * WINDOW:(n + 1) * WINDOW] * inv[n]
        chunks = []
        for c in range(4):
            h = c // 2
            lo = head_out[(c, 0)][:, (0 if h == 0 else LANES):(LANES if h == 0 else 2 * LANES)]
            hi = head_out[(c, 1)][:, (0 if h == 1 else LANES):(LANES if h == 1 else 2 * LANES)]
            chunks.append(jnp.where(half_q == 0, lo, hi))
        o_blocks.append(jnp.concatenate(chunks, axis=1))
    o = jnp.concatenate(o_blocks, axis=0)
    att = _mm(o, w_att_ref[...])

    kprev[...] = kb[TM - WINDOW:]
    kprev_sw[...] = kb_sw[TM - WINDOW:]
    vprev2[...] = vb2[TM - WINDOW:]
    kwin_ref[...] = k[TM - WINDOW:]
    vwin_ref[...] = v[TM - WINDOW:]
    for c in range(n_grp):
        cst_ref[:, c * LANES:(c + 1) * LANES] = useq[c, pl.ds(TM + first, n_taps - 1), :]

    cn = _layer_norm(cbuf[...], cg_ref[...], cb_ref[...])
    cproj = _mm(cn * _sigmoid(cn), w_co_ref[...])

    gate_conv = jnp.concatenate(gate_conv, axis=1)
    row_groups = [slice(0, TM // 2), slice(TM // 2, TM)]
    mixed = []
    for rows in row_groups:
        merged = _sigmoid(gate_att[rows]) * att[rows] + _sigmoid(gate_conv[rows]) * cproj[rows]
        mixed.append(_mm(merged, w_out_ref[...]))
    for n, rows in enumerate(row_groups):
        x1_ref[rows, :] = _layer_norm(alpha * x[rows] + mixed[n], g1_ref[...], b1_ref[...])


def _prompt_ffn_kernel(x1_ref, p_ref, w_up_ref, w_fc_ref, b_fc_ref, w_dn_ref, g2_ref, b2_ref,
                       wpg_ref, wpp_ref, g3_ref, b3_ref,
                       y_ref, fst_ref, upseq, *, alpha):
    i = pl.program_id(0)
    n_slab = upseq.shape[0]
    tail = upseq.shape[1] - TM
    d_ff = w_dn_ref.shape[0]
    n_taps = w_fc_ref.shape[0]

    @pl.when(i == 0)
    def _():
        upseq[:, TM:TM + tail, :] = jnp.zeros((n_slab, tail, LANES), F32)

    upseq[:, 0:tail, :] = upseq[:, TM:TM + tail, :]

    x1 = x1_ref[...]
    xb = _bf(x1)

    def up_proj(c0):
        return [jnp.dot(xb, w_up_ref[:, base:base + FF_CHUNK], preferred_element_type=F32) for base in (c0, d_ff + c0)]

    def conv_act(c0, ups):
        halves = []
        for base, up in zip((c0, d_ff + c0), ups):
            parts = []
            for s0 in range(0, FF_CHUNK, LANES):
                slab = (base + s0) // LANES
                cols = slice(base + s0, base + s0 + LANES)
                cur = up[:, s0:s0 + LANES]
                upseq[slab, tail:tail + TM, :] = cur
                hcv = b_fc_ref[:, cols] + cur * w_fc_ref[n_taps - 1:n_taps, cols]
                for j in range(n_taps - 1):
                    hcv = hcv + upseq[slab, pl.ds(tail - (n_taps - 1) + j, TM), :] * w_fc_ref[j:j + 1, cols]
                parts.append(hcv)
            halves.append(jnp.concatenate(parts, axis=1))
        return _bf(_gelu(halves[0]) * halves[1])

    starts = list(range(0, d_ff, FF_CHUNK))
    ups = up_proj(starts[0])
    f = None
    e_proj = None
    acts = []
    for n, c0 in enumerate(starts):
        if n + 1 < len(starts):
            nxt = up_proj(starts[n + 1])
        else:
            nxt = None
            e_proj = _mm(p_ref[...], wpp_ref[...])
        acts.append(conv_act(c0, ups))
        if len(acts) == DOWN_GROUP or nxt is None:
            g0 = c0 + FF_CHUNK - len(acts) * FF_CHUNK
            part = jnp.dot(jnp.concatenate(acts, axis=1), w_dn_ref[g0:c0 + FF_CHUNK, :], preferred_element_type=F32)
            f = part if f is None else f + part
            acts = []
        ups = nxt

    for s in range(n_slab):
        fst_ref[:, s * LANES:(s + 1) * LANES] = upseq[s, pl.ds(TM + tail - (n_taps - 1), n_taps - 1), :]

    row_groups = [slice(0, TM // 2), slice(TM // 2, TM)]
    ys = _ffn_tail_rows(x1, f, e_proj, g2_ref[...], b2_ref[...], wpg_ref[...], g3_ref[...], b3_ref[...], alpha, row_groups)
    for rows, y in zip(row_groups, ys):
        y_ref[rows, :] = y


def _rows_time_major(ref):
    return jnp.concatenate([ref[:, t, :] for t in range(ref.shape[1])], axis=0)


def _sample_proj_kernel(rope_ref, x_ref, st_ref, w_in_ref, w_dw_ref, b_dw_ref, cg_ref, cb_ref, w_co_ref,
                        q_ref, k_ref, v_ref, cst_ref, ga_ref, ct_ref):
    nb, steps, d_model = x_ref.shape
    n_taps, c_conv = w_dw_ref.shape
    n_hist = n_taps - 1
    qkv_cols, glu_cols, ga_cols, gc_cols = _in_proj_columns(d_model, c_conv)
    x = _rows_time_major(x_ref)
    xb = _bf(x)

    pos = (PAST_LEN + lax.broadcasted_iota(jnp.int32, (steps, LANES), 0)).astype(F32)
    ang = pos * rope_ref[0:1, :]
    cos8 = jnp.cos(ang)
    sin8 = jnp.where(_first_half_of_head((steps, LANES)), -jnp.sin(ang), jnp.sin(ang))
    cos = jnp.concatenate([jnp.broadcast_to(cos8[t:t + 1], (nb, LANES)) for t in range(steps)], axis=0)
    sin = jnp.concatenate([jnp.broadcast_to(sin8[t:t + 1], (nb, LANES)) for t in range(steps)], axis=0)

    qkv = jnp.dot(xb, w_in_ref[:, qkv_cols[0]:qkv_cols[1]], preferred_element_type=F32)
    scale = HEAD_DIM ** -0.5
    q = jnp.concatenate([_rope(qkv[:, c * LANES:(c + 1) * LANES], cos, sin) * scale
                         for c in range(ATT_WIDTH // LANES)], axis=1)
    k = _rope(qkv[:, ATT_WIDTH:ATT_WIDTH + KV_WIDTH], cos, sin)
    v = qkv[:, ATT_WIDTH + KV_WIDTH:ATT_WIDTH + 2 * KV_WIDTH]
    for t in range(steps):
        rows = slice(t * nb, (t + 1) * nb)
        q_ref[:, t, :] = q[rows]
        k_ref[:, t, :] = k[rows]
        v_ref[:, t, :] = v[rows]

    glu = jnp.dot(xb, w_in_ref[:, glu_cols[0]:glu_cols[1]], preferred_element_type=F32)
    u = glu[:, 0:c_conv] * _sigmoid(glu[:, c_conv:2 * c_conv])
    seq = [st_ref[r] for r in range(n_hist)]
    seq += [u[t * nb:(t + 1) * nb] for t in range(steps)]
    outs = []
    for t in range(steps):
        acc = jnp.broadcast_to(b_dw_ref[...], (nb, c_conv))
        for j in range(n_taps):
            acc = acc + seq[t + j] * w_dw_ref[j:j + 1, :]
        outs.append(acc)
    for r in range(n_hist):
        cst_ref[r] = seq[steps + r]
    cn = _layer_norm(jnp.concatenate(outs, axis=0), cg_ref[...], cb_ref[...])
    cproj = _mm(cn * _sigmoid(cn), w_co_ref[...])

    gates = jnp.dot(xb, w_in_ref[:, ga_cols[0]:gc_cols[1]], preferred_element_type=F32)
    ga_ref[...] = _sigmoid(gates[:, 0:d_model])
    ct_ref[...] = _sigmoid(gates[:, d_model:2 * d_model]) * cproj


def _sample_attn_kernel(q_ref, kn_ref, vn_ref, ckt_ref, cvt_ref, sink_ref, o_ref, kwint_ref, vwint_ref, *, steps):
    n_seq = ckt_ref.shape[0]
    wc = ckt_ref.shape[2]
    half8 = _lane_half((steps, LANES))
    n_rows = N_HEADS * steps
    row_t = lax.broadcasted_iota(jnp.int32, (n_rows, 2 * wc), 0) % steps
    col = lax.broadcasted_iota(jnp.int32, (n_rows, 2 * wc), 1)
    new_t = col - (2 * wc - steps)
    valid = jnp.where(col < wc, col - row_t - (wc - WINDOW) - 1, jnp.minimum(new_t, row_t - new_t)) >= 0
    sink = sink_ref[:, 0:1]
    lane_w = lax.broadcasted_iota(jnp.int32, (LANES, wc), 1)
    keep_old = lane_w < (wc - steps)
    pad = jnp.zeros((wc - steps, LANES), F32)

    def one(s):
        rows = pl.ds(pl.multiple_of(s * steps, steps), steps)
        qs = q_ref[rows, :]
        blocks = []
        for c in range(ATT_WIDTH // LANES):
            qc = qs[:, c * LANES:(c + 1) * LANES]
            qc_sw = _swap_halves(qc)
            h = c // 2
            for a in range(2):
                blocks.append(jnp.where(half8 == h, qc if a == h else qc_sw, 0.0))
        lhs = _bf(jnp.concatenate(blocks, axis=0))
        kct, vct = ckt_ref[s], cvt_ref[s]
        knt = jnp.concatenate([pad, kn_ref[rows, :]], axis=0).T
        vnt = jnp.concatenate([pad, vn_ref[rows, :]], axis=0).T
        sc = jnp.dot(lhs, _bf(jnp.concatenate([kct, knt], axis=1)), preferred_element_type=F32)
        e, r = _softmax_with_sink(jnp.where(valid, sc, NEG), sink)
        out = _mm_t(_bf(e), _bf(jnp.concatenate([vct, vnt], axis=1))) * r
        chunks = []
        for c in range(ATT_WIDTH // LANES):
            h = c // 2
            lo = out[(2 * c) * steps:(2 * c + 1) * steps]
            hi = out[(2 * c + 1) * steps:(2 * c + 2) * steps]
            lo = lo if h == 0 else _swap_halves(lo)
            hi = hi if h == 1 else _swap_halves(hi)
            chunks.append(jnp.where(half8 == 0, lo, hi))
        o_ref[rows, :] = jnp.concatenate(chunks, axis=1)
        kwint_ref[s] = jnp.where(keep_old, pltpu.roll(kct, wc - steps, 1), knt)
        vwint_ref[s] = jnp.where(keep_old, pltpu.roll(vct, wc - steps, 1), vnt)

    def group(g, carry):
        for n in range(ATT_UNROLL):
            one(g * ATT_UNROLL + n)
        return carry

    lax.fori_loop(0, n_seq // ATT_UNROLL, group, 0)


def _sample_ffn_kernel(o_ref, ga_ref, ct_ref, x_ref, p_ref, st_ref, w_att_ref, w_out_ref, g1_ref, b1_ref,
                       w_up_ref, w_fc_ref, b_fc_ref, w_dn_ref, g2_ref, b2_ref, wpg_ref, wpp_ref, g3_ref, b3_ref,
                       y_ref, fst_ref, *, alpha):
    nb, steps, d_model = x_ref.shape
    d_ff = w_dn_ref.shape[0]
    n_taps = w_fc_ref.shape[0]
    n_hist = n_taps - 1

    o = _rows_time_major(o_ref)
    x = _rows_time_major(x_ref)
    p = _rows_time_major(p_ref)
    att = _mm(o, w_att_ref[...])
    mixed = _mm(ga_ref[...] * att + ct_ref[...], w_out_ref[...])
    x1 = _layer_norm(alpha * x + mixed, g1_ref[...], b1_ref[...])
    xb = _bf(x1)

    def up_proj(c0):
        return [jnp.dot(xb, w_up_ref[:, base:base + FF_CHUNK], preferred_element_type=F32) for base in (c0, d_ff + c0)]

    def conv_act(c0, ups):
        halves = []
        for base, up in zip((c0, d_ff + c0), ups):
            cols = slice(base, base + FF_CHUNK)
            seq = [st_ref[:, r, cols] for r in range(n_hist)]
            seq += [up[t * nb:(t + 1) * nb] for t in range(steps)]
            outs = []
            for t in range(steps):
                acc = jnp.broadcast_to(b_fc_ref[:, cols], (nb, FF_CHUNK))
                for j in range(n_taps):
                    acc = acc + seq[t + j] * w_fc_ref[j:j + 1, cols]
                outs.append(acc)
            for r in range(n_hist):
                fst_ref[:, r, cols] = seq[steps + r]
            halves.append(jnp.concatenate(outs, axis=0))
        return _bf(_gelu(halves[0]) * halves[1])

    starts = list(range(0, d_ff, FF_CHUNK))
    ups = up_proj(starts[0])
    f = None
    for n, c0 in enumerate(starts):
        nxt = up_proj(starts[n + 1]) if n + 1 < len(starts) else None
        part = jnp.dot(conv_act(c0, ups), w_dn_ref[c0:c0 + FF_CHUNK, :], preferred_element_type=F32)
        f = part if f is None else f + part
        ups = nxt

    e_proj = _mm(p, wpp_ref[...])
    y, = _ffn_tail_rows(x1, f, e_proj, g2_ref[...], b2_ref[...], wpg_ref[...], g3_ref[...], b3_ref[...], alpha,
                        [slice(0, steps * nb)])
    for t in range(steps):
        y_ref[:, t, :] = y[t * nb:(t + 1) * nb]


def _const_spec(shape):
    zeros = (0,) * len(shape)
    return pl.BlockSpec(shape, lambda i: zeros, pipeline_mode=pl.Buffered(1))


def _const_out_spec(shape):
    zeros = (0,) * len(shape)
    return pl.BlockSpec(shape, lambda i: zeros)


def _row_spec(rows, width):
    return pl.BlockSpec((rows, width), lambda i: (i, 0))


def _params():
    return pltpu.CompilerParams(dimension_semantics=("arbitrary",), vmem_limit_bytes=VMEM_LIMIT)


def _rope_rows():
    half = HEAD_DIM // 2
    inv = ROPE_THETA ** (-jnp.arange(half, dtype=F32) / half)
    return jnp.broadcast_to(jnp.tile(inv, LANES // half)[None, :], (8, LANES))


def _row2d(v):
    return v.reshape(1, -1)


def _prompt_layer(x, p, rope_rows, w, alpha):
    t_len, d_model = x.shape
    n_taps, c_conv = w["w_dw"].shape
    d_ff = w["w_down"].shape[0]
    f_taps = w["w_fconv"].shape[0]
    conv_tail = -(-(n_taps - 1) // 8) * 8
    ffn_tail = -(-(f_taps - 1) // 8) * 8
    grid = (t_len // TM,)

    mixer_in = [w["sinks"], rope_rows, x, w["w_in"], w["w_attn_out"], w["w_dw"], _row2d(w["b_dw"]),
                _row2d(w["conv_ln_g"]), _row2d(w["conv_ln_b"]), w["w_conv_out"], w["w_out"],
                _row2d(w["ln1_g"]), _row2d(w["ln1_b"])]
    mixer_specs = [pl.BlockSpec(memory_space=pltpu.SMEM), _const_spec(rope_rows.shape), _row_spec(TM, d_model)]
    mixer_specs += [_const_spec(a.shape) for a in mixer_in[3:]]
    x1, kwin, vwin, cst = pl.pallas_call(
        functools.partial(_prompt_mixer_kernel, alpha=alpha),
        grid=grid,
        in_specs=mixer_specs,
        out_specs=[_row_spec(TM, d_model), _const_out_spec((WINDOW, LANES)), _const_out_spec((WINDOW, LANES)),
                   _const_out_spec((n_taps - 1, c_conv))],
        out_shape=[jax.ShapeDtypeStruct((t_len, d_model), F32), jax.ShapeDtypeStruct((WINDOW, LANES), F32),
                   jax.ShapeDtypeStruct((WINDOW, LANES), F32), jax.ShapeDtypeStruct((n_taps - 1, c_conv), F32)],
        scratch_shapes=[pltpu.VMEM((WINDOW, LANES), BF16), pltpu.VMEM((WINDOW, LANES), BF16),
                        pltpu.VMEM((WINDOW, 2 * LANES), BF16),
                        pltpu.VMEM((c_conv // LANES, TM + conv_tail, LANES), F32),
                        pltpu.VMEM((TM, c_conv), F32),
                        pltpu.VMEM((2, TM, LANES), F32)],
        compiler_params=_params(),
        name="prompt_mixer",
    )(*mixer_in)

    ffn_in = [x1, p, w["w_up"], w["w_fconv"], _row2d(w["b_fconv"]), w["w_down"], _row2d(w["ln2_g"]), _row2d(w["ln2_b"]),
              w["w_ple_gate"], w["w_ple_proj"], _row2d(w["ln3_g"]), _row2d(w["ln3_b"])]
    ffn_specs = [_row_spec(TM, d_model), _row_spec(TM, p.shape[1])] + [_const_spec(a.shape) for a in ffn_in[2:]]
    y, fst = pl.pallas_call(
        functools.partial(_prompt_ffn_kernel, alpha=alpha),
        grid=grid,
        in_specs=ffn_specs,
        out_specs=[_row_spec(TM, d_model), _const_out_spec((f_taps - 1, 2 * d_ff))],
        out_shape=[jax.ShapeDtypeStruct((t_len, d_model), F32), jax.ShapeDtypeStruct((f_taps - 1, 2 * d_ff), F32)],
        scratch_shapes=[pltpu.VMEM((2 * d_ff // LANES, TM + ffn_tail, LANES), F32)],
        compiler_params=_params(),
        name="prompt_ffn",
    )(*ffn_in)
    return y, kwin, vwin, cst, fst


def _sample_layer(x, p, cache_kt, cache_vt, st_conv, st_ffn, rope_rows, w, alpha):
    n_seq, steps, d_model = x.shape
    wc = cache_kt.shape[2]
    n_tok = n_seq * steps
    blk_rows = SEQ_BLK * steps
    grid = (n_seq // SEQ_BLK,)
    seq_spec = lambda mid, width: pl.BlockSpec((SEQ_BLK, mid, width), lambda i: (i, 0, 0))
    hist_spec = pl.BlockSpec((st_conv.shape[0], SEQ_BLK, st_conv.shape[2]), lambda i: (0, i, 0))

    proj_in = [rope_rows, x, st_conv, w["w_in"], w["w_dw"], _row2d(w["b_dw"]), _row2d(w["conv_ln_g"]),
               _row2d(w["conv_ln_b"]), w["w_conv_out"]]
    proj_specs = [_const_spec(rope_rows.shape), seq_spec(steps, d_model), hist_spec]
    proj_specs += [_const_spec(a.shape) for a in proj_in[3:]]
    q, kn, vn, cst, ga, ct = pl.pallas_call(
        _sample_proj_kernel,
        grid=grid,
        in_specs=proj_specs,
        out_specs=[seq_spec(steps, ATT_WIDTH), seq_spec(steps, KV_WIDTH), seq_spec(steps, KV_WIDTH), hist_spec,
                   _row_spec(blk_rows, d_model), _row_spec(blk_rows, d_model)],
        out_shape=[jax.ShapeDtypeStruct((n_seq, steps, ATT_WIDTH), F32), jax.ShapeDtypeStruct((n_seq, steps, KV_WIDTH), F32),
                   jax.ShapeDtypeStruct((n_seq, steps, KV_WIDTH), F32), jax.ShapeDtypeStruct(st_conv.shape, F32),
                   jax.ShapeDtypeStruct((n_tok, d_model), F32), jax.ShapeDtypeStruct((n_tok, d_model), F32)],
        compiler_params=_params(),
        name="sample_proj",
    )(*proj_in)

    sink_rows = jnp.broadcast_to(jnp.repeat(w["sinks"], steps)[:, None], (N_HEADS * steps, LANES))
    att_rows = ATT_BLK * steps
    cache_spec = pl.BlockSpec((ATT_BLK, KV_WIDTH, wc), lambda i: (i, 0, 0))
    o, kwin_t, vwin_t = pl.pallas_call(
        functools.partial(_sample_attn_kernel, steps=steps),
        grid=(n_seq // ATT_BLK,),
        in_specs=[_row_spec(att_rows, ATT_WIDTH), _row_spec(att_rows, KV_WIDTH), _row_spec(att_rows, KV_WIDTH),
                  cache_spec, cache_spec, _const_spec(sink_rows.shape)],
        out_specs=[_row_spec(att_rows, ATT_WIDTH), cache_spec, cache_spec],
        out_shape=[jax.ShapeDtypeStruct((n_tok, ATT_WIDTH), F32), jax.ShapeDtypeStruct(cache_kt.shape, F32),
                   jax.ShapeDtypeStruct(cache_vt.shape, F32)],
        compiler_params=_params(),
        name="sample_attn",
    )(q.reshape(n_tok, ATT_WIDTH), kn.reshape(n_tok, KV_WIDTH), vn.reshape(n_tok, KV_WIDTH), cache_kt, cache_vt, sink_rows)

    ffn_in = [o.reshape(n_seq, steps, ATT_WIDTH), ga, ct, x, p, st_ffn, w["w_attn_out"], w["w_out"], _row2d(w["ln1_g"]),
              _row2d(w["ln1_b"]), w["w_up"], w["w_fconv"], _row2d(w["b_fconv"]), w["w_down"], _row2d(w["ln2_g"]),
              _row2d(w["ln2_b"]), w["w_ple_gate"], w["w_ple_proj"], _row2d(w["ln3_g"]), _row2d(w["ln3_b"])]
    ffn_specs = [seq_spec(steps, ATT_WIDTH), _row_spec(blk_rows, d_model), _row_spec(blk_rows, d_model),
                 seq_spec(steps, d_model), seq_spec(steps, p.shape[2]), seq_spec(*st_ffn.shape[1:])]
    ffn_specs += [_const_spec(a.shape) for a in ffn_in[6:]]
    y, fst = pl.pallas_call(
        functools.partial(_sample_ffn_kernel, alpha=alpha),
        grid=grid,
        in_specs=ffn_specs,
        out_specs=[seq_spec(steps, d_model), seq_spec(*st_ffn.shape[1:])],
        out_shape=[jax.ShapeDtypeStruct(x.shape, F32), jax.ShapeDtypeStruct(st_ffn.shape, F32)],
        compiler_params=_params(),
        name="sample_ffn",
    )(*ffn_in)
    return y, kwin_t, vwin_t, cst, fst


def kernel(x_prompt, x_sample, cache_k, cache_v, state_conv, state_ffn_conv, p_prompt, p_sample, w_in, sinks, w_attn_out, w_dw, b_dw, conv_ln_g, conv_ln_b, w_conv_out, w_out, ln1_g, ln1_b, w_up, w_fconv, b_fconv, w_down, ln2_g, ln2_b, w_ple_gate, w_ple_proj, ln3_g, ln3_b):
    depth = w_in.shape[0]
    bp, t_len, d_model = x_prompt.shape
    n_seq, steps, _ = x_sample.shape
    wc = cache_k.shape[2]
    assert bp == 1 and t_len % TM == 0 and n_seq % SEQ_BLK == 0 and n_seq % ATT_BLK == 0 and ATT_BLK % ATT_UNROLL == 0
    assert KV_WIDTH == LANES and wc == WINDOW and wc == LANES and steps == 8
    assert cache_k.shape[3:] == (N_KV_HEADS, HEAD_DIM)
    alpha = (2 * depth) ** 0.25
    rope_rows = _rope_rows()

    matmul_weights = dict(w_in=w_in, w_attn_out=w_attn_out, w_conv_out=w_conv_out, w_out=w_out, w_up=w_up,
                          w_down=w_down, w_ple_gate=w_ple_gate, w_ple_proj=w_ple_proj)
    other = dict(sinks=sinks, w_dw=w_dw, b_dw=b_dw, conv_ln_g=conv_ln_g, conv_ln_b=conv_ln_b, ln1_g=ln1_g, ln1_b=ln1_b,
                 w_fconv=w_fconv, b_fconv=b_fconv, ln2_g=ln2_g, ln2_b=ln2_b, ln3_g=ln3_g, ln3_b=ln3_b)

    def keys_on_lanes(c):
        return jnp.transpose(c, (0, 2, 3, 1)).reshape(n_seq, KV_WIDTH, wc)

    def keys_on_rows(ct):
        return jnp.transpose(ct.reshape(n_seq, N_KV_HEADS, HEAD_DIM, wc), (0, 3, 1, 2))

    yp = x_prompt.reshape(t_len, d_model)
    ys = x_sample
    outs = [[] for _ in range(8)]
    for l in range(depth):
        w = {name: _bf(a[l]) for name, a in matmul_weights.items()}
        w.update({name: a[l] for name, a in other.items()})
        yp, kp, vp, cp, fp = _prompt_layer(yp, p_prompt[l, 0], rope_rows, w, alpha)
        ys, ks_t, vs_t, cs, fs = _sample_layer(
            ys, p_sample[l], keys_on_lanes(cache_k[l]), keys_on_lanes(cache_v[l]),
            jnp.transpose(state_conv[l], (1, 0, 2)), state_ffn_conv[l], rope_rows, w, alpha)
        kv_shape = (N_KV_HEADS, HEAD_DIM)
        for lst, a in zip(outs, (kp.reshape(1, wc, *kv_shape), vp.reshape(1, wc, *kv_shape), cp[None], fp[None],
                                 keys_on_rows(ks_t), keys_on_rows(vs_t), jnp.transpose(cs, (1, 0, 2)), fs)):
            lst.append(a)
    return (yp.reshape(x_prompt.shape), ys) + tuple(jnp.stack(lst) for lst in outs)
```

```python
import functools
import math

import jax
import jax.numpy as jnp
from jax import lax
from jax.experimental import pallas as pl
from jax.experimental.pallas import tpu as pltpu

F32 = jnp.float32
BF16 = jnp.bfloat16

LANES = 128
HEAD_DIM = 64
N_HEADS = 8
N_KV_HEADS = 2
ATT_WIDTH = N_HEADS * HEAD_DIM
KV_WIDTH = N_KV_HEADS * HEAD_DIM
WINDOW = 128
ROPE_THETA = 10000.0
PAST_LEN = 16384
LN_EPS = 1e-5
NEG = -1e30
GELU_C = math.sqrt(2.0 / math.pi)

TM = 512
SEQ_BLK = 32
ATT_BLK = 16
ATT_UNROLL = 4
FF_CHUNK = 256
DOWN_GROUP = 4
CONV_ROWS = 128
VMEM_LIMIT = 56 * 1024 * 1024


def _in_proj_columns(d_model, c_conv):
    qkv = ATT_WIDTH + 2 * KV_WIDTH
    glu = qkv + 2 * c_conv
    return (0, qkv), (qkv, glu), (glu, glu + d_model), (glu + d_model, glu + 2 * d_model)


def _bf(x):
    return x.astype(BF16)


def _mm(a, w):
    return jnp.dot(_bf(a), w, preferred_element_type=F32)


def _mm_t(a, b):
    return lax.dot_general(a, b, (((1,), (1,)), ((), ())), preferred_element_type=F32)


def _sigmoid(x):
    return 0.5 * (jnp.tanh(0.5 * x) + 1.0)


def _one_plus_tanh(h):
    return jnp.tanh(h) + 1.0


def _gelu(x):
    return 0.5 * x * (1.0 + jnp.tanh(GELU_C * (x + 0.044715 * (x * x * x))))


def _layer_norm(x, g, b):
    mu = jnp.mean(x, axis=-1, keepdims=True)
    d = x - mu
    var = jnp.mean(d * d, axis=-1, keepdims=True)
    return d * lax.rsqrt(var + LN_EPS) * g + b


def _lane_half(shape):
    return (lax.broadcasted_iota(jnp.int32, shape, len(shape) - 1) // HEAD_DIM) % 2


def _first_half_of_head(shape):
    lane = lax.broadcasted_iota(jnp.int32, shape, len(shape) - 1)
    return (lane % HEAD_DIM) < (HEAD_DIM // 2)


def _rope(x, cos, sin_signed):
    partner = jnp.where(_first_half_of_head(x.shape), pltpu.roll(x, LANES - HEAD_DIM // 2, 1),
                        pltpu.roll(x, HEAD_DIM // 2, 1))
    return x * cos + partner * sin_signed


def _swap_halves(x):
    return pltpu.roll(x, HEAD_DIM, 1)


def _softmax_with_sink(s, sink):
    m = jnp.maximum(jnp.max(s, axis=-1, keepdims=True), sink)
    e = jnp.exp(s - m)
    den = jnp.sum(e, axis=-1, keepdims=True) + jnp.exp(sink - m)
    return e, 1.0 / den


def _ffn_tail_rows(x1, f, e_proj, ln2_g, ln2_b, wpg, ln3_g, ln3_b, alpha, row_groups):
    x2 = []
    gate = []
    for rows in row_groups:
        x2.append(_layer_norm(alpha * x1[rows] + f[rows], ln2_g, ln2_b))
        gate.append(_mm(x2[-1], wpg))
    return [_layer_norm(alpha * x2[n] + _sigmoid(gate[n]) * e_proj[rows], ln3_g, ln3_b)
            for n, rows in enumerate(row_groups)]


def _prompt_mixer_kernel(sinks_ref, rope_ref, x_ref, w_in_ref, w_att_ref, w_dw_ref, b_dw_ref,
                         cg_ref, cb_ref, w_co_ref, w_out_ref, g1_ref, b1_ref,
                         x1_ref, kwin_ref, vwin_ref, cst_ref,
                         kprev, kprev_sw, vprev2, useq, cbuf, rtab, *, alpha):
    i = pl.program_id(0)
    tail = useq.shape[1] - TM
    n_grp = useq.shape[0]
    n_taps = w_dw_ref.shape[0]
    first = tail - (n_taps - 1)
    d_model = x_ref.shape[1]
    c_conv = n_grp * LANES
    qkv_cols, glu_cols, ga_cols, gc_cols = _in_proj_columns(d_model, c_conv)
    inv_freq = rope_ref[0:1, :]
    n_blk = TM // WINDOW

    @pl.when(i == 0)
    def _():
        kprev[...] = jnp.zeros_like(kprev)
        kprev_sw[...] = jnp.zeros_like(kprev_sw)
        vprev2[...] = jnp.zeros_like(vprev2)
        useq[:, TM:TM + tail, :] = jnp.zeros((n_grp, tail, LANES), F32)
        ang = lax.broadcasted_iota(jnp.int32, (TM, LANES), 0).astype(F32) * inv_freq
        rtab[0] = jnp.cos(ang)
        rtab[1] = jnp.sin(ang)

    useq[:, 0:tail, :] = useq[:, TM:TM + tail, :]

    x = x_ref[...]
    xb = _bf(x)

    ang_b = (i * TM).astype(F32) * inv_freq
    cos_b, sin_b = jnp.cos(ang_b), jnp.sin(ang_b)
    cos_r, sin_r = rtab[0], rtab[1]
    cos = cos_b * cos_r - sin_b * sin_r
    sin = sin_b * cos_r + cos_b * sin_r
    sin = jnp.where(_first_half_of_head((TM, LANES)), -sin, sin)

    half_c = c_conv // 2

    def glu_proj(hh):
        lo = glu_cols[0] + hh * half_c
        return (jnp.dot(xb, w_in_ref[:, lo:lo + half_c], preferred_element_type=F32),
                jnp.dot(xb, w_in_ref[:, lo + c_conv:lo + c_conv + half_c], preferred_element_type=F32))

    def glu_to_history(hh, ab):
        u = ab[0] * _one_plus_tanh(ab[1])
        for cc in range(half_c // LANES):
            useq[hh * (half_c // LANES) + cc, tail:tail + TM, :] = u[:, cc * LANES:(cc + 1) * LANES]

    def conv_group(c):
        cols = slice(c * LANES, (c + 1) * LANES)
        for r0 in range(0, TM, CONV_ROWS):
            acc = jnp.broadcast_to(b_dw_ref[:, cols], (CONV_ROWS, LANES))
            for j in range(n_taps):
                acc = acc + useq[c, pl.ds(first + j + r0, CONV_ROWS), :] * w_dw_ref[j:j + 1, cols]
            cbuf[r0:r0 + CONV_ROWS, cols] = acc

    glu0 = glu_proj(0)
    glu1 = glu_proj(1)
    glu_to_history(0, glu0)
    qkv = jnp.dot(xb, w_in_ref[:, qkv_cols[0]:qkv_cols[1]], preferred_element_type=F32)
    for c in range(0, n_grp // 2):
        conv_group(c)
    glu_to_history(1, glu1)
    for c in range(n_grp // 2, n_grp):
        conv_group(c)

    scale = HEAD_DIM ** -0.5
    q = [_rope(qkv[:, c * LANES:(c + 1) * LANES], cos, sin) * scale for c in range(ATT_WIDTH // LANES)]
    k = _rope(qkv[:, ATT_WIDTH:ATT_WIDTH + KV_WIDTH], cos, sin)
    v = qkv[:, ATT_WIDTH + KV_WIDTH:ATT_WIDTH + 2 * KV_WIDTH]
    kb, kb_sw = _bf(k), _bf(_swap_halves(k))
    vb2 = jnp.concatenate([_bf(v), _bf(_swap_halves(v))], axis=1)

    half_q = _lane_half((WINDOW, LANES))
    own_block = (lax.broadcasted_iota(jnp.int32, (WINDOW, WINDOW), 1)
                 <= lax.broadcasted_iota(jnp.int32, (WINDOW, WINDOW), 0))
    stack_same = [(c, a) for c in range(4) for a in range(2) if a == c // 2]
    stack_swap = [(c, a) for c in range(4) for a in range(2) if a != c // 2]
    gc_step = (gc_cols[1] - gc_cols[0]) // n_blk

    o_blocks = []
    gate_conv, gate_att = [], []
    for bi in range(n_blk):
        r0 = bi * WINDOW
        if bi == 0:
            kp, kp_sw, vp2 = kprev[...], kprev_sw[...], vprev2[...]
        else:
            kp, kp_sw, vp2 = kb[r0 - WINDOW:r0], kb_sw[r0 - WINDOW:r0], vb2[r0 - WINDOW:r0]
        kk = jnp.concatenate([kp, kb[r0:r0 + WINDOW]], axis=0)
        kk_sw = jnp.concatenate([kp_sw, kb_sw[r0:r0 + WINDOW]], axis=0)
        vv2 = jnp.concatenate([vp2, vb2[r0:r0 + WINDOW]], axis=0)
        scores = []
        for heads, keys in ((stack_same, kk), (stack_swap, kk_sw)):
            qm = jnp.concatenate(
                [_bf(jnp.where(half_q == a, q[c][r0:r0 + WINDOW], 0.0)) for c, a in heads], axis=0)
            scores.append(_mm_t(qm, keys))
        head_out = {}
        for heads, s_all in ((stack_same, scores[0]), (stack_swap, scores[1])):
            probs, inv = [], []
            for n, (c, a) in enumerate(heads):
                s_prev = s_all[n * WINDOW:(n + 1) * WINDOW, 0:WINDOW]
                if bi == 0:
                    s_prev = jnp.where(i == 0, NEG, s_prev)
                s = jnp.where(own_block, s_all[n * WINDOW:(n + 1) * WINDOW, WINDOW:2 * WINDOW], s_prev)
                e, r = _softmax_with_sink(s, sinks_ref[2 * c + a])
                probs.append(_bf(jnp.concatenate([jnp.where(own_block, 0.0, e), jnp.where(own_block, e, 0.0)], axis=1)))
                inv.append(r)
            gates, first_col = (gate_conv, gc_cols[0]) if heads is stack_same else (gate_att, ga_cols[0])
            lo_col = first_col + bi * gc_step
            gates.append(jnp.dot(xb, w_in_ref[:, lo_col:lo_col + gc_step], preferred_element_type=F32))
            pv = jnp.dot(jnp.concatenate(probs, axis=0), vv2, preferred_element_type=F32)
            for n, (c, a) in enumerate(heads):
                head_out[(c, a)] = (pv[n * WINDOW:(n + 1) * WINDOW], inv[n])
        chunks = []
        for c in range(4):
            h = c // 2
            (pv_lo, inv_lo), (pv_hi, inv_hi) = head_out[(c, 0)], head_out[(c, 1)]
            lo = pv_lo[:, (0 if h == 0 else LANES):(LANES if h == 0 else 2 * LANES)] * inv_lo
            hi = pv_hi[:, (0 if h == 1 else LANES):(LANES if h == 1 else 2 * LANES)] * inv_hi
            chunks.append(jnp.where(half_q == 0, lo, hi))
        o_blocks.append(jnp.concatenate(chunks, axis=1))
    o = jnp.concatenate(o_blocks, axis=0)
    att = _mm(o, w_att_ref[...])

    kprev[...] = kb[TM - WINDOW:]
    kprev_sw[...] = kb_sw[TM - WINDOW:]
    vprev2[...] = vb2[TM - WINDOW:]
    kwin_ref[...] = k[TM - WINDOW:]
    vwin_ref[...] = v[TM - WINDOW:]
    for c in range(n_grp):
        cst_ref[:, c * LANES:(c + 1) * LANES] = useq[c, pl.ds(TM + first, n_taps - 1), :]

    cn = _layer_norm(cbuf[...], cg_ref[...], cb_ref[...])
    cproj = _mm(cn * _sigmoid(cn), w_co_ref[...])

    gate_conv = jnp.concatenate(gate_conv, axis=1)
    gate_att = jnp.concatenate(gate_att, axis=1)
    row_groups = [slice(0, TM // 2), slice(TM // 2, TM)]
    mixed = []
    for rows in row_groups:
        merged2 = _one_plus_tanh(gate_att[rows]) * att[rows] + _one_plus_tanh(gate_conv[rows]) * cproj[rows]
        mixed.append(_mm(merged2, w_out_ref[...]))
    for n, rows in enumerate(row_groups):
        x1_ref[rows, :] = _layer_norm(alpha * x[rows] + 0.5 * mixed[n], g1_ref[...], b1_ref[...])


def _prompt_ffn_kernel(x1_ref, p_ref, w_up_ref, w_fc_ref, b_fc_ref, w_dn_ref, g2_ref, b2_ref,
                       wpg_ref, wpp_ref, g3_ref, b3_ref,
                       y_ref, fst_ref, upseq, *, alpha):
    i = pl.program_id(0)
    n_slab = upseq.shape[0]
    tail = upseq.shape[1] - TM
    d_ff = w_dn_ref.shape[0]
    n_taps = w_fc_ref.shape[0]

    @pl.when(i == 0)
    def _():
        upseq[:, TM:TM + tail, :] = jnp.zeros((n_slab, tail, LANES), F32)

    upseq[:, 0:tail, :] = upseq[:, TM:TM + tail, :]

    x1 = x1_ref[...]
    xb = _bf(x1)

    def up_proj(c0):
        return [jnp.dot(xb, w_up_ref[:, base:base + FF_CHUNK], preferred_element_type=F32) for base in (c0, d_ff + c0)]

    def conv_act(c0, ups):
        halves = []
        for base, up in zip((c0, d_ff + c0), ups):
            parts = []
            for s0 in range(0, FF_CHUNK, LANES):
                slab = (base + s0) // LANES
                cols = slice(base + s0, base + s0 + LANES)
                cur = up[:, s0:s0 + LANES]
                upseq[slab, tail:tail + TM, :] = cur
                hcv = b_fc_ref[:, cols] + cur * w_fc_ref[n_taps - 1:n_taps, cols]
                for j in range(n_taps - 1):
                    hcv = hcv + upseq[slab, pl.ds(tail - (n_taps - 1) + j, TM), :] * w_fc_ref[j:j + 1, cols]
                parts.append(hcv)
            halves.append(jnp.concatenate(parts, axis=1))
        return _bf(_gelu(halves[0]) * halves[1])

    starts = list(range(0, d_ff, FF_CHUNK))
    ups = up_proj(starts[0])
    f = None
    e_proj = None
    acts = []
    for n, c0 in enumerate(starts):
        if n + 1 < len(starts):
            nxt = up_proj(starts[n + 1])
        else:
            nxt = None
            e_proj = _mm(p_ref[...], wpp_ref[...])
        acts.append(conv_act(c0, ups))
        if len(acts) == DOWN_GROUP or nxt is None:
            g0 = c0 + FF_CHUNK - len(acts) * FF_CHUNK
            part = jnp.dot(jnp.concatenate(acts, axis=1), w_dn_ref[g0:c0 + FF_CHUNK, :], preferred_element_type=F32)
            f = part if f is None else f + part
            acts = []
        ups = nxt

    for s in range(n_slab):
        fst_ref[:, s * LANES:(s + 1) * LANES] = upseq[s, pl.ds(TM + tail - (n_taps - 1), n_taps - 1), :]

    row_groups = [slice(0, TM // 2), slice(TM // 2, TM)]
    ys = _ffn_tail_rows(x1, f, e_proj, g2_ref[...], b2_ref[...], wpg_ref[...], g3_ref[...], b3_ref[...], alpha, row_groups)
    for rows, y in zip(row_groups, ys):
        y_ref[rows, :] = y


def _rows_time_major(ref):
    return jnp.concatenate([ref[:, t, :] for t in range(ref.shape[1])], axis=0)


def _sample_proj_kernel(rope_ref, x_ref, st_ref, w_in_ref, w_dw_ref, b_dw_ref, cg_ref, cb_ref, w_co_ref,
                        q_ref, k_ref, v_ref, cst_ref, ga_ref, ct_ref):
    nb, steps, d_model = x_ref.shape
    n_taps, c_conv = w_dw_ref.shape
    n_hist = n_taps - 1
    qkv_cols, glu_cols, ga_cols, gc_cols = _in_proj_columns(d_model, c_conv)
    x = _rows_time_major(x_ref)
    xb = _bf(x)

    pos = (PAST_LEN + lax.broadcasted_iota(jnp.int32, (steps, LANES), 0)).astype(F32)
    ang = pos * rope_ref[0:1, :]
    cos8 = jnp.cos(ang)
    sin8 = jnp.where(_first_half_of_head((steps, LANES)), -jnp.sin(ang), jnp.sin(ang))
    cos = jnp.concatenate([jnp.broadcast_to(cos8[t:t + 1], (nb, LANES)) for t in range(steps)], axis=0)
    sin = jnp.concatenate([jnp.broadcast_to(sin8[t:t + 1], (nb, LANES)) for t in range(steps)], axis=0)

    qkv = jnp.dot(xb, w_in_ref[:, qkv_cols[0]:qkv_cols[1]], preferred_element_type=F32)
    scale = HEAD_DIM ** -0.5
    q = jnp.concatenate([_rope(qkv[:, c * LANES:(c + 1) * LANES], cos, sin) * scale
                         for c in range(ATT_WIDTH // LANES)], axis=1)
    k = _rope(qkv[:, ATT_WIDTH:ATT_WIDTH + KV_WIDTH], cos, sin)
    v = qkv[:, ATT_WIDTH + KV_WIDTH:ATT_WIDTH + 2 * KV_WIDTH]
    for t in range(steps):
        rows = slice(t * nb, (t + 1) * nb)
        q_ref[:, t, :] = q[rows]
        k_ref[:, t, :] = k[rows]
        v_ref[:, t, :] = v[rows]

    glu = jnp.dot(xb, w_in_ref[:, glu_cols[0]:glu_cols[1]], preferred_element_type=F32)
    u = glu[:, 0:c_conv] * _one_plus_tanh(glu[:, c_conv:2 * c_conv])
    seq = [st_ref[r] for r in range(n_hist)]
    seq += [u[t * nb:(t + 1) * nb] for t in range(steps)]
    outs = []
    for t in range(steps):
        acc = jnp.broadcast_to(b_dw_ref[...], (nb, c_conv))
        for j in range(n_taps):
            acc = acc + seq[t + j] * w_dw_ref[j:j + 1, :]
        outs.append(acc)
    for r in range(n_hist):
        cst_ref[r] = seq[steps + r]
    cn = _layer_norm(jnp.concatenate(outs, axis=0), cg_ref[...], cb_ref[...])
    cproj = _mm(cn * _sigmoid(cn), w_co_ref[...])

    gates = jnp.dot(xb, w_in_ref[:, ga_cols[0]:gc_cols[1]], preferred_element_type=F32)
    ga_ref[...] = 0.5 * _one_plus_tanh(gates[:, 0:d_model])
    ct_ref[...] = 0.5 * _one_plus_tanh(gates[:, d_model:2 * d_model]) * cproj


def _sample_attn_kernel(q_ref, kn_ref, vn_ref, ckt_ref, cvt_ref, sink_ref, o_ref, kwint_ref, vwint_ref, *, steps):
    n_seq = ckt_ref.shape[0]
    wc = ckt_ref.shape[2]
    half8 = _lane_half((steps, LANES))
    n_rows = N_HEADS * steps
    row_t = lax.broadcasted_iota(jnp.int32, (n_rows, 2 * wc), 0) % steps
    col = lax.broadcasted_iota(jnp.int32, (n_rows, 2 * wc), 1)
    new_t = col - (2 * wc - steps)
    valid = jnp.where(col < wc, col - row_t - (wc - WINDOW) - 1, jnp.minimum(new_t, row_t - new_t)) >= 0
    sink = sink_ref[:, 0:1]
    lane_w = lax.broadcasted_iota(jnp.int32, (LANES, wc), 1)
    keep_old = lane_w < (wc - steps)
    pad = jnp.zeros((wc - steps, LANES), F32)

    def one(s):
        rows = pl.ds(pl.multiple_of(s * steps, steps), steps)
        qs = q_ref[rows, :]
        blocks = []
        for c in range(ATT_WIDTH // LANES):
            qc = qs[:, c * LANES:(c + 1) * LANES]
            qc_sw = _swap_halves(qc)
            h = c // 2
            for a in range(2):
                blocks.append(jnp.where(half8 == h, qc if a == h else qc_sw, 0.0))
        lhs = _bf(jnp.concatenate(blocks, axis=0))
        kct, vct = ckt_ref[s], cvt_ref[s]
        knt = jnp.concatenate([pad, kn_ref[rows, :]], axis=0).T
        vnt = jnp.concatenate([pad, vn_ref[rows, :]], axis=0).T
        sc = jnp.dot(lhs, _bf(jnp.concatenate([kct, knt], axis=1)), preferred_element_type=F32)
        e, r = _softmax_with_sink(jnp.where(valid, sc, NEG), sink)
        out = _mm_t(_bf(e), _bf(jnp.concatenate([vct, vnt], axis=1))) * r
        chunks = []
        for c in range(ATT_WIDTH // LANES):
            h = c // 2
            lo = out[(2 * c) * steps:(2 * c + 1) * steps]
            hi = out[(2 * c + 1) * steps:(2 * c + 2) * steps]
            lo = lo if h == 0 else _swap_halves(lo)
            hi = hi if h == 1 else _swap_halves(hi)
            chunks.append(jnp.where(half8 == 0, lo, hi))
        o_ref[rows, :] = jnp.concatenate(chunks, axis=1)
        kwint_ref[s] = jnp.where(keep_old, pltpu.roll(kct, wc - steps, 1), knt)
        vwint_ref[s] = jnp.where(keep_old, pltpu.roll(vct, wc - steps, 1), vnt)

    def group(g, carry):
        for n in range(ATT_UNROLL):
            one(g * ATT_UNROLL + n)
        return carry

    lax.fori_loop(0, n_seq // ATT_UNROLL, group, 0)


def _sample_ffn_kernel(o_ref, ga_ref, ct_ref, x_ref, p_ref, st_ref, w_att_ref, w_out_ref, g1_ref, b1_ref,
                       w_up_ref, w_fc_ref, b_fc_ref, w_dn_ref, g2_ref, b2_ref, wpg_ref, wpp_ref, g3_ref, b3_ref,
                       y_ref, fst_ref, *, alpha):
    nb, steps, d_model = x_ref.shape
    d_ff = w_dn_ref.shape[0]
    n_taps = w_fc_ref.shape[0]
    n_hist = n_taps - 1

    o = _rows_time_major(o_ref)
    x = _rows_time_major(x_ref)
    p = _rows_time_major(p_ref)
    att = _mm(o, w_att_ref[...])
    mixed = _mm(ga_ref[...] * att + ct_ref[...], w_out_ref[...])
    x1 = _layer_norm(alpha * x + mixed, g1_ref[...], b1_ref[...])
    xb = _bf(x1)

    def up_proj(c0):
        return [jnp.dot(xb, w_up_ref[:, base:base + FF_CHUNK], preferred_element_type=F32) for base in (c0, d_ff + c0)]

    def conv_act(c0, ups):
        halves = []
        for base, up in zip((c0, d_ff + c0), ups):
            cols = slice(base, base + FF_CHUNK)
            seq = [st_ref[:, r, cols] for r in range(n_hist)]
            seq += [up[t * nb:(t + 1) * nb] for t in range(steps)]
            outs = []
            for t in range(steps):
                acc = jnp.broadcast_to(b_fc_ref[:, cols], (nb, FF_CHUNK))
                for j in range(n_taps):
                    acc = acc + seq[t + j] * w_fc_ref[j:j + 1, cols]
                outs.append(acc)
            for r in range(n_hist):
                fst_ref[:, r, cols] = seq[steps + r]
            halves.append(jnp.concatenate(outs, axis=0))
        return _bf(_gelu(halves[0]) * halves[1])

    starts = list(range(0, d_ff, FF_CHUNK))
    ups = up_proj(starts[0])
    f = None
    acts = []
    for n, c0 in enumerate(starts):
        nxt = up_proj(starts[n + 1]) if n + 1 < len(starts) else None
        acts.append(conv_act(c0, ups))
        if len(acts) == DOWN_GROUP or nxt is None:
            g0 = c0 + FF_CHUNK - len(acts) * FF_CHUNK
            part = jnp.dot(jnp.concatenate(acts, axis=1), w_dn_ref[g0:c0 + FF_CHUNK, :], preferred_element_type=F32)
            f = part if f is None else f + part
            acts = []
        ups = nxt

    e_proj = _mm(p, wpp_ref[...])
    y, = _ffn_tail_rows(x1, f, e_proj, g2_ref[...], b2_ref[...], wpg_ref[...], g3_ref[...], b3_ref[...], alpha,
                        [slice(0, steps * nb)])
    for t in range(steps):
        y_ref[:, t, :] = y[t * nb:(t + 1) * nb]


def _const_spec(shape):
    zeros = (0,) * len(shape)
    return pl.BlockSpec(shape, lambda i: zeros, pipeline_mode=pl.Buffered(1))


def _const_out_spec(shape):
    zeros = (0,) * len(shape)
    return pl.BlockSpec(shape, lambda i: zeros)


def _row_spec(rows, width):
    return pl.BlockSpec((rows, width), lambda i: (i, 0))


def _params():
    return pltpu.CompilerParams(dimension_semantics=("arbitrary",), vmem_limit_bytes=VMEM_LIMIT)


def _rope_rows():
    half = HEAD_DIM // 2
    inv = ROPE_THETA ** (-jnp.arange(half, dtype=F32) / half)
    return jnp.broadcast_to(jnp.tile(inv, LANES // half)[None, :], (8, LANES))


def _row2d(v):
    return v.reshape(1, -1)


def _prompt_layer(x, p, rope_rows, w, alpha):
    t_len, d_model = x.shape
    n_taps, c_conv = w["w_dw"].shape
    d_ff = w["w_down"].shape[0]
    f_taps = w["w_fconv"].shape[0]
    conv_tail = -(-(n_taps - 1) // 8) * 8
    ffn_tail = -(-(f_taps - 1) // 8) * 8
    grid = (t_len // TM,)

    mixer_in = [w["sinks"], rope_rows, x, w["w_in"], w["w_attn_out"], w["w_dw"], _row2d(w["b_dw"]),
                _row2d(w["conv_ln_g"]), _row2d(w["conv_ln_b"]), w["w_conv_out"], w["w_out"],
                _row2d(w["ln1_g"]), _row2d(w["ln1_b"])]
    mixer_specs = [pl.BlockSpec(memory_space=pltpu.SMEM), _const_spec(rope_rows.shape), _row_spec(TM, d_model)]
    mixer_specs += [_const_spec(a.shape) for a in mixer_in[3:]]
    x1, kwin, vwin, cst = pl.pallas_call(
        functools.partial(_prompt_mixer_kernel, alpha=alpha),
        grid=grid,
        in_specs=mixer_specs,
        out_specs=[_row_spec(TM, d_model), _const_out_spec((WINDOW, LANES)), _const_out_spec((WINDOW, LANES)),
                   _const_out_spec((n_taps - 1, c_conv))],
        out_shape=[jax.ShapeDtypeStruct((t_len, d_model), F32), jax.ShapeDtypeStruct((WINDOW, LANES), F32),
                   jax.ShapeDtypeStruct((WINDOW, LANES), F32), jax.ShapeDtypeStruct((n_taps - 1, c_conv), F32)],
        scratch_shapes=[pltpu.VMEM((WINDOW, LANES), BF16), pltpu.VMEM((WINDOW, LANES), BF16),
                        pltpu.VMEM((WINDOW, 2 * LANES), BF16),
                        pltpu.VMEM((c_conv // LANES, TM + conv_tail, LANES), F32),
                        pltpu.VMEM((TM, c_conv), F32),
                        pltpu.VMEM((2, TM, LANES), F32)],
        compiler_params=_params(),
        name="prompt_mixer",
    )(*mixer_in)

    ffn_in = [x1, p, w["w_up"], w["w_fconv"], _row2d(w["b_fconv"]), w["w_down"], _row2d(w["ln2_g"]), _row2d(w["ln2_b"]),
              w["w_ple_gate"], w["w_ple_proj"], _row2d(w["ln3_g"]), _row2d(w["ln3_b"])]
    ffn_specs = [_row_spec(TM, d_model), _row_spec(TM, p.shape[1])] + [_const_spec(a.shape) for a in ffn_in[2:]]
    y, fst = pl.pallas_call(
        functools.partial(_prompt_ffn_kernel, alpha=alpha),
        grid=grid,
        in_specs=ffn_specs,
        out_specs=[_row_spec(TM, d_model), _const_out_spec((f_taps - 1, 2 * d_ff))],
        out_shape=[jax.ShapeDtypeStruct((t_len, d_model), F32), jax.ShapeDtypeStruct((f_taps - 1, 2 * d_ff), F32)],
        scratch_shapes=[pltpu.VMEM((2 * d_ff // LANES, TM + ffn_tail, LANES), F32)],
        compiler_params=_params(),
        name="prompt_ffn",
    )(*ffn_in)
    return y, kwin, vwin, cst, fst


def _sample_layer(x, p, cache_kt, cache_vt, st_conv, st_ffn, rope_rows, w, alpha):
    n_seq, steps, d_model = x.shape
    wc = cache_kt.shape[2]
    n_tok = n_seq * steps
    blk_rows = SEQ_BLK * steps
    grid = (n_seq // SEQ_BLK,)
    seq_spec = lambda mid, width: pl.BlockSpec((SEQ_BLK, mid, width), lambda i: (i, 0, 0))
    hist_spec = pl.BlockSpec((st_conv.shape[0], SEQ_BLK, st_conv.shape[2]), lambda i: (0, i, 0))

    proj_in = [rope_rows, x, st_conv, w["w_in"], w["w_dw"], _row2d(w["b_dw"]), _row2d(w["conv_ln_g"]),
               _row2d(w["conv_ln_b"]), w["w_conv_out"]]
    proj_specs = [_const_spec(rope_rows.shape), seq_spec(steps, d_model), hist_spec]
    proj_specs += [_const_spec(a.shape) for a in proj_in[3:]]
    q, kn, vn, cst, ga, ct = pl.pallas_call(
        _sample_proj_kernel,
        grid=grid,
        in_specs=proj_specs,
        out_specs=[seq_spec(steps, ATT_WIDTH), seq_spec(steps, KV_WIDTH), seq_spec(steps, KV_WIDTH), hist_spec,
                   _row_spec(blk_rows, d_model), _row_spec(blk_rows, d_model)],
        out_shape=[jax.ShapeDtypeStruct((n_seq, steps, ATT_WIDTH), F32), jax.ShapeDtypeStruct((n_seq, steps, KV_WIDTH), F32),
                   jax.ShapeDtypeStruct((n_seq, steps, KV_WIDTH), F32), jax.ShapeDtypeStruct(st_conv.shape, F32),
                   jax.ShapeDtypeStruct((n_tok, d_model), F32), jax.ShapeDtypeStruct((n_tok, d_model), F32)],
        compiler_params=_params(),
        name="sample_proj",
    )(*proj_in)

    sink_rows = jnp.broadcast_to(jnp.repeat(w["sinks"], steps)[:, None], (N_HEADS * steps, LANES))
    att_rows = ATT_BLK * steps
    cache_spec = pl.BlockSpec((ATT_BLK, KV_WIDTH, wc), lambda i: (i, 0, 0))
    o, kwin_t, vwin_t = pl.pallas_call(
        functools.partial(_sample_attn_kernel, steps=steps),
        grid=(n_seq // ATT_BLK,),
        in_specs=[_row_spec(att_rows, ATT_WIDTH), _row_spec(att_rows, KV_WIDTH), _row_spec(att_rows, KV_WIDTH),
                  cache_spec, cache_spec, _const_spec(sink_rows.shape)],
        out_specs=[_row_spec(att_rows, ATT_WIDTH), cache_spec, cache_spec],
        out_shape=[jax.ShapeDtypeStruct((n_tok, ATT_WIDTH), F32), jax.ShapeDtypeStruct(cache_kt.shape, F32),
                   jax.ShapeDtypeStruct(cache_vt.shape, F32)],
        compiler_params=_params(),
        name="sample_attn",
    )(q.reshape(n_tok, ATT_WIDTH), kn.reshape(n_tok, KV_WIDTH), vn.reshape(n_tok, KV_WIDTH), cache_kt, cache_vt, sink_rows)

    ffn_in = [o.reshape(n_seq, steps, ATT_WIDTH), ga, ct, x, p, st_ffn, w["w_attn_out"], w["w_out"], _row2d(w["ln1_g"]),
              _row2d(w["ln1_b"]), w["w_up"], w["w_fconv"], _row2d(w["b_fconv"]), w["w_down"], _row2d(w["ln2_g"]),
              _row2d(w["ln2_b"]), w["w_ple_gate"], w["w_ple_proj"], _row2d(w["ln3_g"]), _row2d(w["ln3_b"])]
    ffn_specs = [seq_spec(steps, ATT_WIDTH), _row_spec(blk_rows, d_model), _row_spec(blk_rows, d_model),
                 seq_spec(steps, d_model), seq_spec(steps, p.shape[2]), seq_spec(*st_ffn.shape[1:])]
    ffn_specs += [_const_spec(a.shape) for a in ffn_in[6:]]
    y, fst = pl.pallas_call(
        functools.partial(_sample_ffn_kernel, alpha=alpha),
        grid=grid,
        in_specs=ffn_specs,
        out_specs=[seq_spec(steps, d_model), seq_spec(*st_ffn.shape[1:])],
        out_shape=[jax.ShapeDtypeStruct(x.shape, F32), jax.ShapeDtypeStruct(st_ffn.shape, F32)],
        compiler_params=_params(),
        name="sample_ffn",
    )(*ffn_in)
    return y, kwin_t, vwin_t, cst, fst


def kernel(x_prompt, x_sample, cache_k, cache_v, state_conv, state_ffn_conv, p_prompt, p_sample, w_in, sinks, w_attn_out, w_dw, b_dw, conv_ln_g, conv_ln_b, w_conv_out, w_out, ln1_g, ln1_b, w_up, w_fconv, b_fconv, w_down, ln2_g, ln2_b, w_ple_gate, w_ple_proj, ln3_g, ln3_b):
    depth = w_in.shape[0]
    bp, t_len, d_model = x_prompt.shape
    n_seq, steps, _ = x_sample.shape
    wc = cache_k.shape[2]
    assert bp == 1 and t_len % TM == 0 and n_seq % SEQ_BLK == 0 and n_seq % ATT_BLK == 0 and ATT_BLK % ATT_UNROLL == 0
    assert KV_WIDTH == LANES and wc == WINDOW and wc == LANES and steps == 8
    assert cache_k.shape[3:] == (N_KV_HEADS, HEAD_DIM)
    alpha = (2 * depth) ** 0.25
    rope_rows = _rope_rows()
    c_conv = w_dw.shape[2]
    qkv_cols = _in_proj_columns(d_model, c_conv)[0]
    gated_col_scale = jnp.where(jnp.arange(w_in.shape[2]) < qkv_cols[1], 1.0, 0.5).astype(F32)

    matmul_weights = dict(w_in=w_in, w_attn_out=w_attn_out, w_conv_out=w_conv_out, w_out=w_out, w_up=w_up,
                          w_down=w_down, w_ple_gate=w_ple_gate, w_ple_proj=w_ple_proj)
    other = dict(sinks=sinks, w_dw=w_dw, b_dw=b_dw, conv_ln_g=conv_ln_g, conv_ln_b=conv_ln_b, ln1_g=ln1_g, ln1_b=ln1_b,
                 w_fconv=w_fconv, b_fconv=b_fconv, ln2_g=ln2_g, ln2_b=ln2_b, ln3_g=ln3_g, ln3_b=ln3_b)

    def keys_on_lanes(c):
        return jnp.transpose(c, (0, 2, 3, 1)).reshape(n_seq, KV_WIDTH, wc)

    def keys_on_rows(ct):
        return jnp.transpose(ct.reshape(n_seq, N_KV_HEADS, HEAD_DIM, wc), (0, 3, 1, 2))

    yp = x_prompt.reshape(t_len, d_model)
    ys = x_sample
    outs = [[] for _ in range(8)]
    for l in range(depth):
        w = {name: _bf(a[l]) for name, a in matmul_weights.items()}
        w["w_in"] = _bf(w_in[l] * gated_col_scale)
        w.update({name: a[l] for name, a in other.items()})
        yp, kp, vp, cp, fp = _prompt_layer(yp, p_prompt[l, 0], rope_rows, w, alpha)
        ys, ks_t, vs_t, cs, fs = _sample_layer(
            ys, p_sample[l], keys_on_lanes(cache_k[l]), keys_on_lanes(cache_v[l]),
            jnp.transpose(state_conv[l], (1, 0, 2)), state_ffn_conv[l], rope_rows, w, alpha)
        kv_shape = (N_KV_HEADS, HEAD_DIM)
        for lst, a in zip(outs, (kp.reshape(1, wc, *kv_shape), vp.reshape(1, wc, *kv_shape), cp[None], fp[None],
                                 keys_on_rows(ks_t), keys_on_rows(vs_t), jnp.transpose(cs, (1, 0, 2)), fs)):
            lst.append(a)
    return (yp.reshape(x_prompt.shape), ys) + tuple(jnp.stack(lst) for lst in outs)
```

```python
import functools
import math

import jax
import jax.numpy as jnp
from jax import lax
from jax.experimental import pallas as pl
from jax.experimental.pallas import tpu as pltpu

F32 = jnp.float32
BF16 = jnp.bfloat16

LANES = 128
HEAD_DIM = 64
N_HEADS = 8
N_KV_HEADS = 2
ATT_WIDTH = N_HEADS * HEAD_DIM
KV_WIDTH = N_KV_HEADS * HEAD_DIM
WINDOW = 128
ROPE_THETA = 10000.0
PAST_LEN = 16384
LN_EPS = 1e-5
NEG = -1e30
GELU_C = math.sqrt(2.0 / math.pi)

TM = 512
SEQ_BLK = 32
ATT_BLK = 32
ATT_UNROLL = 4
FF_CHUNK = 256
DOWN_GROUP = 4
CONV_ROWS = 128
VMEM_LIMIT = 56 * 1024 * 1024


def _in_proj_columns(d_model, c_conv):
    qkv = ATT_WIDTH + 2 * KV_WIDTH
    glu = qkv + 2 * c_conv
    return (0, qkv), (qkv, glu), (glu, glu + d_model), (glu + d_model, glu + 2 * d_model)


def _bf(x):
    return x.astype(BF16)


def _mm(a, w):
    return jnp.dot(_bf(a), w, preferred_element_type=F32)


def _mm_t(a, b):
    return lax.dot_general(a, b, (((1,), (1,)), ((), ())), preferred_element_type=F32)


def _sigmoid(x):
    return 0.5 * (jnp.tanh(0.5 * x) + 1.0)


def _one_plus_tanh(h):
    return jnp.tanh(h) + 1.0


def _gelu(x):
    return 0.5 * x * (1.0 + jnp.tanh(GELU_C * (x + 0.044715 * (x * x * x))))


def _layer_norm(x, g, b):
    mu = jnp.mean(x, axis=-1, keepdims=True)
    d = x - mu
    var = jnp.mean(d * d, axis=-1, keepdims=True)
    return d * lax.rsqrt(var + LN_EPS) * g + b


def _lane_half(shape):
    return (lax.broadcasted_iota(jnp.int32, shape, len(shape) - 1) // HEAD_DIM) % 2


def _first_half_of_head(shape):
    lane = lax.broadcasted_iota(jnp.int32, shape, len(shape) - 1)
    return (lane % HEAD_DIM) < (HEAD_DIM // 2)


def _rope(x, cos, sin_signed):
    partner = jnp.where(_first_half_of_head(x.shape), pltpu.roll(x, LANES - HEAD_DIM // 2, 1),
                        pltpu.roll(x, HEAD_DIM // 2, 1))
    return x * cos + partner * sin_signed


def _swap_halves(x):
    return pltpu.roll(x, HEAD_DIM, 1)


def _softmax_with_sink(s, sink):
    m = jnp.maximum(jnp.max(s, axis=-1, keepdims=True), sink)
    e = jnp.exp(s - m)
    den = jnp.sum(e, axis=-1, keepdims=True) + jnp.exp(sink - m)
    return e, 1.0 / den


def _ffn_tail_rows(x1, f, e_proj, ln2_g, ln2_b, wpg, ln3_g, ln3_b, alpha, row_groups):
    x2 = []
    gate = []
    for rows in row_groups:
        x2.append(_layer_norm(alpha * x1[rows] + f[rows], ln2_g, ln2_b))
        gate.append(_mm(x2[-1], wpg))
    return [_layer_norm(alpha * x2[n] + _sigmoid(gate[n]) * e_proj[rows], ln3_g, ln3_b)
            for n, rows in enumerate(row_groups)]


def _prompt_mixer_kernel(sinks_ref, rope_ref, x_ref, w_in_ref, w_att_ref, w_dw_ref, b_dw_ref,
                         cg_ref, cb_ref, w_co_ref, w_out_ref, g1_ref, b1_ref,
                         x1_ref, kwin_ref, vwin_ref, cst_ref,
                         kprev, kprev_sw, vprev2, useq, cbuf, rtab, *, alpha):
    i = pl.program_id(0)
    tail = useq.shape[1] - TM
    n_grp = useq.shape[0]
    n_taps = w_dw_ref.shape[0]
    first = tail - (n_taps - 1)
    d_model = x_ref.shape[1]
    c_conv = n_grp * LANES
    qkv_cols, glu_cols, ga_cols, gc_cols = _in_proj_columns(d_model, c_conv)
    inv_freq = rope_ref[0:1, :]
    n_blk = TM // WINDOW

    @pl.when(i == 0)
    def _():
        kprev[...] = jnp.zeros_like(kprev)
        kprev_sw[...] = jnp.zeros_like(kprev_sw)
        vprev2[...] = jnp.zeros_like(vprev2)
        useq[:, TM:TM + tail, :] = jnp.zeros((n_grp, tail, LANES), F32)
        ang = lax.broadcasted_iota(jnp.int32, (TM, LANES), 0).astype(F32) * inv_freq
        rtab[0] = jnp.cos(ang)
        rtab[1] = jnp.sin(ang)

    useq[:, 0:tail, :] = useq[:, TM:TM + tail, :]

    x = x_ref[...]
    xb = _bf(x)

    ang_b = (i * TM).astype(F32) * inv_freq
    cos_b, sin_b = jnp.cos(ang_b), jnp.sin(ang_b)
    cos_r, sin_r = rtab[0], rtab[1]
    cos = cos_b * cos_r - sin_b * sin_r
    sin = sin_b * cos_r + cos_b * sin_r
    sin = jnp.where(_first_half_of_head((TM, LANES)), -sin, sin)

    half_c = c_conv // 2

    def glu_proj(hh):
        lo = glu_cols[0] + hh * half_c
        return (jnp.dot(xb, w_in_ref[:, lo:lo + half_c], preferred_element_type=F32),
                jnp.dot(xb, w_in_ref[:, lo + c_conv:lo + c_conv + half_c], preferred_element_type=F32))

    def glu_to_history(hh, ab):
        u = ab[0] * _one_plus_tanh(ab[1])
        for cc in range(half_c // LANES):
            useq[hh * (half_c // LANES) + cc, tail:tail + TM, :] = u[:, cc * LANES:(cc + 1) * LANES]

    def conv_group(c):
        cols = slice(c * LANES, (c + 1) * LANES)
        for r0 in range(0, TM, CONV_ROWS):
            acc = jnp.broadcast_to(b_dw_ref[:, cols], (CONV_ROWS, LANES))
            for j in range(n_taps):
                acc = acc + useq[c, pl.ds(first + j + r0, CONV_ROWS), :] * w_dw_ref[j:j + 1, cols]
            cbuf[r0:r0 + CONV_ROWS, cols] = acc

    glu0 = glu_proj(0)
    glu1 = glu_proj(1)
    glu_to_history(0, glu0)
    qkv = jnp.dot(xb, w_in_ref[:, qkv_cols[0]:qkv_cols[1]], preferred_element_type=F32)
    for c in range(0, n_grp // 2):
        conv_group(c)
    glu_to_history(1, glu1)
    for c in range(n_grp // 2, n_grp):
        conv_group(c)

    scale = HEAD_DIM ** -0.5
    q = [_rope(qkv[:, c * LANES:(c + 1) * LANES], cos, sin) * scale for c in range(ATT_WIDTH // LANES)]
    k = _rope(qkv[:, ATT_WIDTH:ATT_WIDTH + KV_WIDTH], cos, sin)
    v = qkv[:, ATT_WIDTH + KV_WIDTH:ATT_WIDTH + 2 * KV_WIDTH]
    kb, kb_sw = _bf(k), _bf(_swap_halves(k))
    vb2 = jnp.concatenate([_bf(v), _bf(_swap_halves(v))], axis=1)

    half_q = _lane_half((WINDOW, LANES))
    own_block = (lax.broadcasted_iota(jnp.int32, (WINDOW, WINDOW), 1)
                 <= lax.broadcasted_iota(jnp.int32, (WINDOW, WINDOW), 0))
    stack_same = [(c, a) for c in range(4) for a in range(2) if a == c // 2]
    stack_swap = [(c, a) for c in range(4) for a in range(2) if a != c // 2]
    gc_step = (gc_cols[1] - gc_cols[0]) // n_blk

    o_blocks = []
    gate_conv, gate_att = [], []
    for bi in range(n_blk):
        r0 = bi * WINDOW
        if bi == 0:
            kp, kp_sw, vp2 = kprev[...], kprev_sw[...], vprev2[...]
        else:
            kp, kp_sw, vp2 = kb[r0 - WINDOW:r0], kb_sw[r0 - WINDOW:r0], vb2[r0 - WINDOW:r0]
        kk = jnp.concatenate([kp, kb[r0:r0 + WINDOW]], axis=0)
        kk_sw = jnp.concatenate([kp_sw, kb_sw[r0:r0 + WINDOW]], axis=0)
        vv2 = jnp.concatenate([vp2, vb2[r0:r0 + WINDOW]], axis=0)
        scores = []
        for heads, keys in ((stack_same, kk), (stack_swap, kk_sw)):
            qm = jnp.concatenate(
                [_bf(jnp.where(half_q == a, q[c][r0:r0 + WINDOW], 0.0)) for c, a in heads], axis=0)
            scores.append(_mm_t(qm, keys))
        head_out = {}
        for heads, s_all in ((stack_same, scores[0]), (stack_swap, scores[1])):
            probs, inv = [], []
            for n, (c, a) in enumerate(heads):
                s_prev = s_all[n * WINDOW:(n + 1) * WINDOW, 0:WINDOW]
                if bi == 0:
                    s_prev = jnp.where(i == 0, NEG, s_prev)
                s = jnp.where(own_block, s_all[n * WINDOW:(n + 1) * WINDOW, WINDOW:2 * WINDOW], s_prev)
                e, r = _softmax_with_sink(s, sinks_ref[2 * c + a])
                probs.append(_bf(jnp.concatenate([jnp.where(own_block, 0.0, e), jnp.where(own_block, e, 0.0)], axis=1)))
                inv.append(r)
            gates, first_col = (gate_conv, gc_cols[0]) if heads is stack_same else (gate_att, ga_cols[0])
            lo_col = first_col + bi * gc_step
            gates.append(jnp.dot(xb, w_in_ref[:, lo_col:lo_col + gc_step], preferred_element_type=F32))
            pv = jnp.dot(jnp.concatenate(probs, axis=0), vv2, preferred_element_type=F32)
            for n, (c, a) in enumerate(heads):
                head_out[(c, a)] = (pv[n * WINDOW:(n + 1) * WINDOW], inv[n])
        chunks = []
        for c in range(4):
            h = c // 2
            (pv_lo, inv_lo), (pv_hi, inv_hi) = head_out[(c, 0)], head_out[(c, 1)]
            lo = pv_lo[:, (0 if h == 0 else LANES):(LANES if h == 0 else 2 * LANES)] * inv_lo
            hi = pv_hi[:, (0 if h == 1 else LANES):(LANES if h == 1 else 2 * LANES)] * inv_hi
            chunks.append(jnp.where(half_q == 0, lo, hi))
        o_blocks.append(jnp.concatenate(chunks, axis=1))
    o = jnp.concatenate(o_blocks, axis=0)
    att = _mm(o, w_att_ref[...])

    kprev[...] = kb[TM - WINDOW:]
    kprev_sw[...] = kb_sw[TM - WINDOW:]
    vprev2[...] = vb2[TM - WINDOW:]
    kwin_ref[...] = k[TM - WINDOW:]
    vwin_ref[...] = v[TM - WINDOW:]
    for c in range(n_grp):
        cst_ref[:, c * LANES:(c + 1) * LANES] = useq[c, pl.ds(TM + first, n_taps - 1), :]

    cn = _layer_norm(cbuf[...], cg_ref[...], cb_ref[...])
    cproj = _mm(cn * _sigmoid(cn), w_co_ref[...])

    gate_conv = jnp.concatenate(gate_conv, axis=1)
    gate_att = jnp.concatenate(gate_att, axis=1)
    row_groups = [slice(0, TM // 2), slice(TM // 2, TM)]
    mixed = []
    for rows in row_groups:
        merged2 = _one_plus_tanh(gate_att[rows]) * att[rows] + _one_plus_tanh(gate_conv[rows]) * cproj[rows]
        mixed.append(_mm(merged2, w_out_ref[...]))
    for n, rows in enumerate(row_groups):
        x1_ref[rows, :] = _layer_norm(alpha * x[rows] + 0.5 * mixed[n], g1_ref[...], b1_ref[...])


def _prompt_ffn_kernel(x1_ref, p_ref, w_up_ref, w_fc_ref, b_fc_ref, w_dn_ref, g2_ref, b2_ref,
                       wpg_ref, wpp_ref, g3_ref, b3_ref,
                       y_ref, fst_ref, upseq, *, alpha):
    i = pl.program_id(0)
    n_slab = upseq.shape[0]
    tail = upseq.shape[1] - TM
    d_ff = w_dn_ref.shape[0]
    n_taps = w_fc_ref.shape[0]

    @pl.when(i == 0)
    def _():
        upseq[:, TM:TM + tail, :] = jnp.zeros((n_slab, tail, LANES), F32)

    upseq[:, 0:tail, :] = upseq[:, TM:TM + tail, :]

    x1 = x1_ref[...]
    xb = _bf(x1)

    def up_proj(c0):
        return [jnp.dot(xb, w_up_ref[:, base:base + FF_CHUNK], preferred_element_type=F32) for base in (c0, d_ff + c0)]

    def conv_act(c0, ups):
        halves = []
        for base, up in zip((c0, d_ff + c0), ups):
            parts = []
            for s0 in range(0, FF_CHUNK, LANES):
                slab = (base + s0) // LANES
                cols = slice(base + s0, base + s0 + LANES)
                cur = up[:, s0:s0 + LANES]
                upseq[slab, tail:tail + TM, :] = cur
                hcv = b_fc_ref[:, cols] + cur * w_fc_ref[n_taps - 1:n_taps, cols]
                for j in range(n_taps - 1):
                    hcv = hcv + upseq[slab, pl.ds(tail - (n_taps - 1) + j, TM), :] * w_fc_ref[j:j + 1, cols]
                parts.append(hcv)
            halves.append(jnp.concatenate(parts, axis=1))
        return _bf(_gelu(halves[0]) * halves[1])

    starts = list(range(0, d_ff, FF_CHUNK))
    ups = up_proj(starts[0])
    f = None
    e_proj = None
    acts = []
    for n, c0 in enumerate(starts):
        if n + 1 < len(starts):
            nxt = up_proj(starts[n + 1])
        else:
            nxt = None
            e_proj = _mm(p_ref[...], wpp_ref[...])
        acts.append(conv_act(c0, ups))
        if len(acts) == DOWN_GROUP or nxt is None:
            g0 = c0 + FF_CHUNK - len(acts) * FF_CHUNK
            part = jnp.dot(jnp.concatenate(acts, axis=1), w_dn_ref[g0:c0 + FF_CHUNK, :], preferred_element_type=F32)
            f = part if f is None else f + part
            acts = []
        ups = nxt

    for s in range(n_slab):
        fst_ref[:, s * LANES:(s + 1) * LANES] = upseq[s, pl.ds(TM + tail - (n_taps - 1), n_taps - 1), :]

    row_groups = [slice(g * (TM // 4), (g + 1) * (TM // 4)) for g in range(4)]
    ys = _ffn_tail_rows(x1, f, e_proj, g2_ref[...], b2_ref[...], wpg_ref[...], g3_ref[...], b3_ref[...], alpha, row_groups)
    for rows, y in zip(row_groups, ys):
        y_ref[rows, :] = y


def _rows_time_major(ref):
    return jnp.concatenate([ref[:, t, :] for t in range(ref.shape[1])], axis=0)


def _sample_proj_kernel(rope_ref, x_ref, st_ref, w_in_ref, w_dw_ref, b_dw_ref, cg_ref, cb_ref, w_co_ref,
                        q_ref, k_ref, v_ref, cst_ref, ga_ref, ct_ref):
    nb, steps, d_model = x_ref.shape
    n_taps, c_conv = w_dw_ref.shape
    n_hist = n_taps - 1
    qkv_cols, glu_cols, ga_cols, gc_cols = _in_proj_columns(d_model, c_conv)
    x = _rows_time_major(x_ref)
    xb = _bf(x)

    pos = (PAST_LEN + lax.broadcasted_iota(jnp.int32, (steps, LANES), 0)).astype(F32)
    ang = pos * rope_ref[0:1, :]
    cos8 = jnp.cos(ang)
    sin8 = jnp.where(_first_half_of_head((steps, LANES)), -jnp.sin(ang), jnp.sin(ang))
    cos = jnp.concatenate([jnp.broadcast_to(cos8[t:t + 1], (nb, LANES)) for t in range(steps)], axis=0)
    sin = jnp.concatenate([jnp.broadcast_to(sin8[t:t + 1], (nb, LANES)) for t in range(steps)], axis=0)

    qkv = jnp.dot(xb, w_in_ref[:, qkv_cols[0]:qkv_cols[1]], preferred_element_type=F32)
    scale = HEAD_DIM ** -0.5
    q = jnp.concatenate([_rope(qkv[:, c * LANES:(c + 1) * LANES], cos, sin) * scale
                         for c in range(ATT_WIDTH // LANES)], axis=1)
    k = _rope(qkv[:, ATT_WIDTH:ATT_WIDTH + KV_WIDTH], cos, sin)
    v = qkv[:, ATT_WIDTH + KV_WIDTH:ATT_WIDTH + 2 * KV_WIDTH]
    for t in range(steps):
        rows = slice(t * nb, (t + 1) * nb)
        q_ref[:, t, :] = q[rows]
        k_ref[:, t, :] = k[rows]
        v_ref[:, t, :] = v[rows]

    glu = jnp.dot(xb, w_in_ref[:, glu_cols[0]:glu_cols[1]], preferred_element_type=F32)
    u = glu[:, 0:c_conv] * _one_plus_tanh(glu[:, c_conv:2 * c_conv])
    seq = [st_ref[r] for r in range(n_hist)]
    seq += [u[t * nb:(t + 1) * nb] for t in range(steps)]
    outs = []
    for t in range(steps):
        acc = jnp.broadcast_to(b_dw_ref[...], (nb, c_conv))
        for j in range(n_taps):
            acc = acc + seq[t + j] * w_dw_ref[j:j + 1, :]
        outs.append(acc)
    for r in range(n_hist):
        cst_ref[r] = seq[steps + r]
    cn = _layer_norm(jnp.concatenate(outs, axis=0), cg_ref[...], cb_ref[...])
    cproj = _mm(cn * _sigmoid(cn), w_co_ref[...])

    gates = jnp.dot(xb, w_in_ref[:, ga_cols[0]:gc_cols[1]], preferred_element_type=F32)
    ga_ref[...] = 0.5 * _one_plus_tanh(gates[:, 0:d_model])
    ct_ref[...] = 0.5 * _one_plus_tanh(gates[:, d_model:2 * d_model]) * cproj


def _sample_attn_kernel(q_ref, kn_ref, vn_ref, ckt_ref, cvt_ref, sink_ref, o_ref, kwint_ref, vwint_ref, *, steps):
    n_seq = ckt_ref.shape[0]
    wc = ckt_ref.shape[2]
    half8 = _lane_half((steps, LANES))
    n_rows = N_HEADS * steps
    row_t = lax.broadcasted_iota(jnp.int32, (n_rows, 2 * wc), 0) % steps
    col = lax.broadcasted_iota(jnp.int32, (n_rows, 2 * wc), 1)
    new_t = col - (2 * wc - steps)
    valid = jnp.where(col < wc, col - row_t - (wc - WINDOW) - 1, jnp.minimum(new_t, row_t - new_t)) >= 0
    sink = sink_ref[:, 0:1]
    lane_w = lax.broadcasted_iota(jnp.int32, (LANES, wc), 1)
    keep_old = lane_w < (wc - steps)
    pad = jnp.zeros((wc - steps, LANES), F32)

    def one(s):
        rows = pl.ds(pl.multiple_of(s * steps, steps), steps)
        qs = q_ref[rows, :]
        blocks = []
        for c in range(ATT_WIDTH // LANES):
            qc = qs[:, c * LANES:(c + 1) * LANES]
            qc_sw = _swap_halves(qc)
            h = c // 2
            for a in range(2):
                blocks.append(jnp.where(half8 == h, qc if a == h else qc_sw, 0.0))
        lhs = _bf(jnp.concatenate(blocks, axis=0))
        kct, vct = ckt_ref[s], cvt_ref[s]
        knt = jnp.concatenate([pad, kn_ref[rows, :]], axis=0).T
        vnt = jnp.concatenate([pad, vn_ref[rows, :]], axis=0).T
        sc = jnp.dot(lhs, _bf(jnp.concatenate([kct, knt], axis=1)), preferred_element_type=F32)
        e, r = _softmax_with_sink(jnp.where(valid, sc, NEG), sink)
        out = _mm_t(_bf(e), _bf(jnp.concatenate([vct, vnt], axis=1))) * r
        chunks = []
        for c in range(ATT_WIDTH // LANES):
            h = c // 2
            lo = out[(2 * c) * steps:(2 * c + 1) * steps]
            hi = out[(2 * c + 1) * steps:(2 * c + 2) * steps]
            lo = lo if h == 0 else _swap_halves(lo)
            hi = hi if h == 1 else _swap_halves(hi)
            chunks.append(jnp.where(half8 == 0, lo, hi))
        o_ref[rows, :] = jnp.concatenate(chunks, axis=1)
        kwint_ref[s] = jnp.where(keep_old, pltpu.roll(kct, wc - steps, 1), knt)
        vwint_ref[s] = jnp.where(keep_old, pltpu.roll(vct, wc - steps, 1), vnt)

    def group(g, carry):
        for n in range(ATT_UNROLL):
            one(g * ATT_UNROLL + n)
        return carry

    lax.fori_loop(0, n_seq // ATT_UNROLL, group, 0)


def _sample_ffn_kernel(o_ref, ga_ref, ct_ref, x_ref, p_ref, st_ref, w_att_ref, w_out_ref, g1_ref, b1_ref,
                       w_up_ref, w_fc_ref, b_fc_ref, w_dn_ref, g2_ref, b2_ref, wpg_ref, wpp_ref, g3_ref, b3_ref,
                       y_ref, fst_ref, *, alpha):
    nb, steps, d_model = x_ref.shape
    d_ff = w_dn_ref.shape[0]
    n_taps = w_fc_ref.shape[0]
    n_hist = n_taps - 1

    o = _rows_time_major(o_ref)
    x = _rows_time_major(x_ref)
    p = _rows_time_major(p_ref)
    att = _mm(o, w_att_ref[...])
    mixed = _mm(ga_ref[...] * att + ct_ref[...], w_out_ref[...])
    x1 = _layer_norm(alpha * x + mixed, g1_ref[...], b1_ref[...])
    xb = _bf(x1)

    def up_proj(c0):
        return [jnp.dot(xb, w_up_ref[:, base:base + FF_CHUNK], preferred_element_type=F32) for base in (c0, d_ff + c0)]

    def conv_act(c0, ups):
        halves = []
        for base, up in zip((c0, d_ff + c0), ups):
            cols = slice(base, base + FF_CHUNK)
            seq = [st_ref[:, r, cols] for r in range(n_hist)]
            seq += [up[t * nb:(t + 1) * nb] for t in range(steps)]
            outs = []
            for t in range(steps):
                acc = jnp.broadcast_to(b_fc_ref[:, cols], (nb, FF_CHUNK))
                for j in range(n_taps):
                    acc = acc + seq[t + j] * w_fc_ref[j:j + 1, cols]
                outs.append(acc)
            for r in range(n_hist):
                fst_ref[:, r, cols] = seq[steps + r]
            halves.append(jnp.concatenate(outs, axis=0))
        return _bf(_gelu(halves[0]) * halves[1])

    starts = list(range(0, d_ff, FF_CHUNK))
    ups = up_proj(starts[0])
    f = None
    acts = []
    for n, c0 in enumerate(starts):
        nxt = up_proj(starts[n + 1]) if n + 1 < len(starts) else None
        acts.append(conv_act(c0, ups))
        if len(acts) == DOWN_GROUP or nxt is None:
            g0 = c0 + FF_CHUNK - len(acts) * FF_CHUNK
            part = jnp.dot(jnp.concatenate(acts, axis=1), w_dn_ref[g0:c0 + FF_CHUNK, :], preferred_element_type=F32)
            f = part if f is None else f + part
            acts = []
        ups = nxt

    e_proj = _mm(p, wpp_ref[...])
    y, = _ffn_tail_rows(x1, f, e_proj, g2_ref[...], b2_ref[...], wpg_ref[...], g3_ref[...], b3_ref[...], alpha,
                        [slice(0, steps * nb)])
    for t in range(steps):
        y_ref[:, t, :] = y[t * nb:(t + 1) * nb]


def _const_spec(shape):
    zeros = (0,) * len(shape)
    return pl.BlockSpec(shape, lambda i: zeros, pipeline_mode=pl.Buffered(1))


def _const_out_spec(shape):
    zeros = (0,) * len(shape)
    return pl.BlockSpec(shape, lambda i: zeros)


def _row_spec(rows, width):
    return pl.BlockSpec((rows, width), lambda i: (i, 0))


def _params():
    return pltpu.CompilerParams(dimension_semantics=("arbitrary",), vmem_limit_bytes=VMEM_LIMIT)


def _rope_rows():
    half = HEAD_DIM // 2
    inv = ROPE_THETA ** (-jnp.arange(half, dtype=F32) / half)
    return jnp.broadcast_to(jnp.tile(inv, LANES // half)[None, :], (8, LANES))


def _row2d(v):
    return v.reshape(1, -1)


def _prompt_layer(x, p, rope_rows, w, alpha):
    t_len, d_model = x.shape
    n_taps, c_conv = w["w_dw"].shape
    d_ff = w["w_down"].shape[0]
    f_taps = w["w_fconv"].shape[0]
    conv_tail = -(-(n_taps - 1) // 8) * 8
    ffn_tail = -(-(f_taps - 1) // 8) * 8
    grid = (t_len // TM,)

    mixer_in = [w["sinks"], rope_rows, x, w["w_in"], w["w_attn_out"], w["w_dw"], _row2d(w["b_dw"]),
                _row2d(w["conv_ln_g"]), _row2d(w["conv_ln_b"]), w["w_conv_out"], w["w_out"],
                _row2d(w["ln1_g"]), _row2d(w["ln1_b"])]
    mixer_specs = [pl.BlockSpec(memory_space=pltpu.SMEM), _const_spec(rope_rows.shape), _row_spec(TM, d_model)]
    mixer_specs += [_const_spec(a.shape) for a in mixer_in[3:]]
    x1, kwin, vwin, cst = pl.pallas_call(
        functools.partial(_prompt_mixer_kernel, alpha=alpha),
        grid=grid,
        in_specs=mixer_specs,
        out_specs=[_row_spec(TM, d_model), _const_out_spec((WINDOW, LANES)), _const_out_spec((WINDOW, LANES)),
                   _const_out_spec((n_taps - 1, c_conv))],
        out_shape=[jax.ShapeDtypeStruct((t_len, d_model), F32), jax.ShapeDtypeStruct((WINDOW, LANES), F32),
                   jax.ShapeDtypeStruct((WINDOW, LANES), F32), jax.ShapeDtypeStruct((n_taps - 1, c_conv), F32)],
        scratch_shapes=[pltpu.VMEM((WINDOW, LANES), BF16), pltpu.VMEM((WINDOW, LANES), BF16),
                        pltpu.VMEM((WINDOW, 2 * LANES), BF16),
                        pltpu.VMEM((c_conv // LANES, TM + conv_tail, LANES), F32),
                        pltpu.VMEM((TM, c_conv), F32),
                        pltpu.VMEM((2, TM, LANES), F32)],
        compiler_params=_params(),
        name="prompt_mixer",
    )(*mixer_in)

    ffn_in = [x1, p, w["w_up"], w["w_fconv"], _row2d(w["b_fconv"]), w["w_down"], _row2d(w["ln2_g"]), _row2d(w["ln2_b"]),
              w["w_ple_gate"], w["w_ple_proj"], _row2d(w["ln3_g"]), _row2d(w["ln3_b"])]
    ffn_specs = [_row_spec(TM, d_model), _row_spec(TM, p.shape[1])] + [_const_spec(a.shape) for a in ffn_in[2:]]
    y, fst = pl.pallas_call(
        functools.partial(_prompt_ffn_kernel, alpha=alpha),
        grid=grid,
        in_specs=ffn_specs,
        out_specs=[_row_spec(TM, d_model), _const_out_spec((f_taps - 1, 2 * d_ff))],
        out_shape=[jax.ShapeDtypeStruct((t_len, d_model), F32), jax.ShapeDtypeStruct((f_taps - 1, 2 * d_ff), F32)],
        scratch_shapes=[pltpu.VMEM((2 * d_ff // LANES, TM + ffn_tail, LANES), F32)],
        compiler_params=_params(),
        name="prompt_ffn",
    )(*ffn_in)
    return y, kwin, vwin, cst, fst


def _sample_layer(x, p, cache_kt, cache_vt, st_conv, st_ffn, rope_rows, w, alpha):
    n_seq, steps, d_model = x.shape
    wc = cache_kt.shape[2]
    n_tok = n_seq * steps
    blk_rows = SEQ_BLK * steps
    grid = (n_seq // SEQ_BLK,)
    seq_spec = lambda mid, width: pl.BlockSpec((SEQ_BLK, mid, width), lambda i: (i, 0, 0))
    hist_spec = pl.BlockSpec((st_conv.shape[0], SEQ_BLK, st_conv.shape[2]), lambda i: (0, i, 0))

    proj_in = [rope_rows, x, st_conv, w["w_in"], w["w_dw"], _row2d(w["b_dw"]), _row2d(w["conv_ln_g"]),
               _row2d(w["conv_ln_b"]), w["w_conv_out"]]
    proj_specs = [_const_spec(rope_rows.shape), seq_spec(steps, d_model), hist_spec]
    proj_specs += [_const_spec(a.shape) for a in proj_in[3:]]
    q, kn, vn, cst, ga, ct = pl.pallas_call(
        _sample_proj_kernel,
        grid=grid,
        in_specs=proj_specs,
        out_specs=[seq_spec(steps, ATT_WIDTH), seq_spec(steps, KV_WIDTH), seq_spec(steps, KV_WIDTH), hist_spec,
                   _row_spec(blk_rows, d_model), _row_spec(blk_rows, d_model)],
        out_shape=[jax.ShapeDtypeStruct((n_seq, steps, ATT_WIDTH), F32), jax.ShapeDtypeStruct((n_seq, steps, KV_WIDTH), F32),
                   jax.ShapeDtypeStruct((n_seq, steps, KV_WIDTH), F32), jax.ShapeDtypeStruct(st_conv.shape, F32),
                   jax.ShapeDtypeStruct((n_tok, d_model), F32), jax.ShapeDtypeStruct((n_tok, d_model), F32)],
        compiler_params=_params(),
        name="sample_proj",
    )(*proj_in)

    sink_rows = jnp.broadcast_to(jnp.repeat(w["sinks"], steps)[:, None], (N_HEADS * steps, LANES))
    att_rows = ATT_BLK * steps
    cache_spec = pl.BlockSpec((ATT_BLK, KV_WIDTH, wc), lambda i: (i, 0, 0))
    o, kwin_t, vwin_t = pl.pallas_call(
        functools.partial(_sample_attn_kernel, steps=steps),
        grid=(n_seq // ATT_BLK,),
        in_specs=[_row_spec(att_rows, ATT_WIDTH), _row_spec(att_rows, KV_WIDTH), _row_spec(att_rows, KV_WIDTH),
                  cache_spec, cache_spec, _const_spec(sink_rows.shape)],
        out_specs=[_row_spec(att_rows, ATT_WIDTH), cache_spec, cache_spec],
        out_shape=[jax.ShapeDtypeStruct((n_tok, ATT_WIDTH), F32), jax.ShapeDtypeStruct(cache_kt.shape, F32),
                   jax.ShapeDtypeStruct(cache_vt.shape, F32)],
        compiler_params=_params(),
        name="sample_attn",
    )(q.reshape(n_tok, ATT_WIDTH), kn.reshape(n_tok, KV_WIDTH), vn.reshape(n_tok, KV_WIDTH), cache_kt, cache_vt, sink_rows)

    ffn_in = [o.reshape(n_seq, steps, ATT_WIDTH), ga, ct, x, p, st_ffn, w["w_attn_out"], w["w_out"], _row2d(w["ln1_g"]),
              _row2d(w["ln1_b"]), w["w_up"], w["w_fconv"], _row2d(w["b_fconv"]), w["w_down"], _row2d(w["ln2_g"]),
              _row2d(w["ln2_b"]), w["w_ple_gate"], w["w_ple_proj"], _row2d(w["ln3_g"]), _row2d(w["ln3_b"])]
    ffn_specs = [seq_spec(steps, ATT_WIDTH), _row_spec(blk_rows, d_model), _row_spec(blk_rows, d_model),
                 seq_spec(steps, d_model), seq_spec(steps, p.shape[2]), seq_spec(*st_ffn.shape[1:])]
    ffn_specs += [_const_spec(a.shape) for a in ffn_in[6:]]
    y, fst = pl.pallas_call(
        functools.partial(_sample_ffn_kernel, alpha=alpha),
        grid=grid,
        in_specs=ffn_specs,
        out_specs=[seq_spec(steps, d_model), seq_spec(*st_ffn.shape[1:])],
        out_shape=[jax.ShapeDtypeStruct(x.shape, F32), jax.ShapeDtypeStruct(st_ffn.shape, F32)],
        compiler_params=_params(),
        name="sample_ffn",
    )(*ffn_in)
    return y, kwin_t, vwin_t, cst, fst


def kernel(x_prompt, x_sample, cache_k, cache_v, state_conv, state_ffn_conv, p_prompt, p_sample, w_in, sinks, w_attn_out, w_dw, b_dw, conv_ln_g, conv_ln_b, w_conv_out, w_out, ln1_g, ln1_b, w_up, w_fconv, b_fconv, w_down, ln2_g, ln2_b, w_ple_gate, w_ple_proj, ln3_g, ln3_b):
    depth = w_in.shape[0]
    bp, t_len, d_model = x_prompt.shape
    n_seq, steps, _ = x_sample.shape
    wc = cache_k.shape[2]
    assert bp == 1 and t_len % TM == 0 and n_seq % SEQ_BLK == 0 and n_seq % ATT_BLK == 0 and ATT_BLK % ATT_UNROLL == 0
    assert KV_WIDTH == LANES and wc == WINDOW and wc == LANES and steps == 8
    assert cache_k.shape[3:] == (N_KV_HEADS, HEAD_DIM)
    alpha = (2 * depth) ** 0.25
    rope_rows = _rope_rows()
    c_conv = w_dw.shape[2]
    qkv_cols = _in_proj_columns(d_model, c_conv)[0]
    gated_col_scale = jnp.where(jnp.arange(w_in.shape[2]) < qkv_cols[1], 1.0, 0.5).astype(F32)

    matmul_weights = dict(w_in=w_in, w_attn_out=w_attn_out, w_conv_out=w_conv_out, w_out=w_out, w_up=w_up,
                          w_down=w_down, w_ple_gate=w_ple_gate, w_ple_proj=w_ple_proj)
    other = dict(sinks=sinks, w_dw=w_dw, b_dw=b_dw, conv_ln_g=conv_ln_g, conv_ln_b=conv_ln_b, ln1_g=ln1_g, ln1_b=ln1_b,
                 w_fconv=w_fconv, b_fconv=b_fconv, ln2_g=ln2_g, ln2_b=ln2_b, ln3_g=ln3_g, ln3_b=ln3_b)

    def keys_on_lanes(c):
        return jnp.transpose(c, (0, 2, 3, 1)).reshape(n_seq, KV_WIDTH, wc)

    def keys_on_rows(ct):
        return jnp.transpose(ct.reshape(n_seq, N_KV_HEADS, HEAD_DIM, wc), (0, 3, 1, 2))

    yp = x_prompt.reshape(t_len, d_model)
    ys = x_sample
    outs = [[] for _ in range(8)]
    for l in range(depth):
        w = {name: _bf(a[l]) for name, a in matmul_weights.items()}
        w["w_in"] = _bf(w_in[l] * gated_col_scale)
        w.update({name: a[l] for name, a in other.items()})
        yp, kp, vp, cp, fp = _prompt_layer(yp, p_prompt[l, 0], rope_rows, w, alpha)
        ys, ks_t, vs_t, cs, fs = _sample_layer(
            ys, p_sample[l], keys_on_lanes(cache_k[l]), keys_on_lanes(cache_v[l]),
            jnp.transpose(state_conv[l], (1, 0, 2)), state_ffn_conv[l], rope_rows, w, alpha)
        kv_shape = (N_KV_HEADS, HEAD_DIM)
        for lst, a in zip(outs, (kp.reshape(1, wc, *kv_shape), vp.reshape(1, wc, *kv_shape), cp[None], fp[None],
                                 keys_on_rows(ks_t), keys_on_rows(vs_t), jnp.transpose(cs, (1, 0, 2)), fs)):
            lst.append(a)
    return (yp.reshape(x_prompt.shape), ys) + tuple(jnp.stack(lst) for lst in outs)
```

```python
import functools
import math

import jax
import jax.numpy as jnp
from jax import lax
from jax.experimental import pallas as pl
from jax.experimental.pallas import tpu as pltpu

F32 = jnp.float32
BF16 = jnp.bfloat16

LANES = 128
HEAD_DIM = 64
N_HEADS = 8
N_KV_HEADS = 2
ATT_WIDTH = N_HEADS * HEAD_DIM
KV_WIDTH = N_KV_HEADS * HEAD_DIM
WINDOW = 128
ROPE_THETA = 10000.0
PAST_LEN = 16384
LN_EPS = 1e-5
NEG = -1e30
GELU_C = math.sqrt(2.0 / math.pi)

TM = 512
SEQ_BLK = 32
ATT_BLK = 32
ATT_UNROLL = 4
FF_CHUNK = 256
DOWN_GROUP = 4
CONV_ROWS = 128
VMEM_LIMIT = 56 * 1024 * 1024


def _in_proj_columns(d_model, c_conv):
    qkv = ATT_WIDTH + 2 * KV_WIDTH
    glu = qkv + 2 * c_conv
    return (0, qkv), (qkv, glu), (glu, glu + d_model), (glu + d_model, glu + 2 * d_model)


def _bf(x):
    return x.astype(BF16)


def _mm(a, w):
    return jnp.dot(_bf(a), w, preferred_element_type=F32)


def _mm_t(a, b):
    return lax.dot_general(a, b, (((1,), (1,)), ((), ())), preferred_element_type=F32)


def _sigmoid(x):
    return 0.5 * (jnp.tanh(0.5 * x) + 1.0)


def _one_plus_tanh(h):
    return jnp.tanh(h) + 1.0


def _gelu(x):
    return 0.5 * x * (1.0 + jnp.tanh(GELU_C * (x + 0.044715 * (x * x * x))))


def _layer_norm(x, g, b):
    mu = jnp.mean(x, axis=-1, keepdims=True)
    d = x - mu
    var = jnp.mean(d * d, axis=-1, keepdims=True)
    return d * lax.rsqrt(var + LN_EPS) * g + b


def _lane_half(shape):
    return (lax.broadcasted_iota(jnp.int32, shape, len(shape) - 1) // HEAD_DIM) % 2


def _first_half_of_head(shape):
    lane = lax.broadcasted_iota(jnp.int32, shape, len(shape) - 1)
    return (lane % HEAD_DIM) < (HEAD_DIM // 2)


def _rope(x, cos, sin_signed):
    partner = jnp.where(_first_half_of_head(x.shape), pltpu.roll(x, LANES - HEAD_DIM // 2, 1),
                        pltpu.roll(x, HEAD_DIM // 2, 1))
    return x * cos + partner * sin_signed


def _swap_halves(x):
    return pltpu.roll(x, HEAD_DIM, 1)


def _softmax_with_sink(s, sink):
    m = jnp.maximum(jnp.max(s, axis=-1, keepdims=True), sink)
    e = jnp.exp(s - m)
    den = jnp.sum(e, axis=-1, keepdims=True) + jnp.exp(sink - m)
    return e, 1.0 / den


def _ffn_tail_rows(x1, f, e_proj, ln2_g, ln2_b, wpg, ln3_g, ln3_b, alpha, row_groups):
    x2 = []
    gate = []
    for rows in row_groups:
        x2.append(_layer_norm(alpha * x1[rows] + f[rows], ln2_g, ln2_b))
        gate.append(_mm(x2[-1], wpg))
    return [_layer_norm(alpha * x2[n] + _sigmoid(gate[n]) * e_proj[rows], ln3_g, ln3_b)
            for n, rows in enumerate(row_groups)]


def _prompt_mixer_kernel(sinks_ref, rope_ref, x_ref, w_in_ref, w_att_ref, w_dw_ref, b_dw_ref,
                         cg_ref, cb_ref, w_co_ref, w_out_ref, g1_ref, b1_ref, *rest, alpha, n_late):
    late_f32, rest = rest[:n_late], rest[n_late:]
    (x1_ref, kwin_ref, vwin_ref, cst_ref), rest = rest[:4], rest[4:]
    late_bf16, (kprev, kprev_sw, vprev2, useq, cbuf, rtab) = rest[:n_late], rest[n_late:]
    i = pl.program_id(0)

    for src, dst in zip(late_f32, late_bf16):
        dst[...] = _bf(src[...])

    tail = useq.shape[1] - TM
    n_grp = useq.shape[0]
    n_taps = w_dw_ref.shape[0]
    first = tail - (n_taps - 1)
    d_model = x_ref.shape[1]
    c_conv = n_grp * LANES
    qkv_cols, glu_cols, ga_cols, gc_cols = _in_proj_columns(d_model, c_conv)
    inv_freq = rope_ref[0:1, :]
    n_blk = TM // WINDOW

    @pl.when(i == 0)
    def _():
        kprev[...] = jnp.zeros_like(kprev)
        kprev_sw[...] = jnp.zeros_like(kprev_sw)
        vprev2[...] = jnp.zeros_like(vprev2)
        useq[:, TM:TM + tail, :] = jnp.zeros((n_grp, tail, LANES), F32)
        ang = lax.broadcasted_iota(jnp.int32, (TM, LANES), 0).astype(F32) * inv_freq
        rtab[0] = jnp.cos(ang)
        rtab[1] = jnp.sin(ang)

    useq[:, 0:tail, :] = useq[:, TM:TM + tail, :]

    x = x_ref[...]
    xb = _bf(x)

    ang_b = (i * TM).astype(F32) * inv_freq
    cos_b, sin_b = jnp.cos(ang_b), jnp.sin(ang_b)
    cos_r, sin_r = rtab[0], rtab[1]
    cos = cos_b * cos_r - sin_b * sin_r
    sin = sin_b * cos_r + cos_b * sin_r
    sin = jnp.where(_first_half_of_head((TM, LANES)), -sin, sin)

    half_c = c_conv // 2

    def glu_proj(hh):
        lo = glu_cols[0] + hh * half_c
        return (jnp.dot(xb, w_in_ref[:, lo:lo + half_c], preferred_element_type=F32),
                jnp.dot(xb, w_in_ref[:, lo + c_conv:lo + c_conv + half_c], preferred_element_type=F32))

    def glu_to_history(hh, ab):
        u = ab[0] * _one_plus_tanh(ab[1])
        for cc in range(half_c // LANES):
            useq[hh * (half_c // LANES) + cc, tail:tail + TM, :] = u[:, cc * LANES:(cc + 1) * LANES]

    def conv_group(c):
        cols = slice(c * LANES, (c + 1) * LANES)
        for r0 in range(0, TM, CONV_ROWS):
            acc = jnp.broadcast_to(b_dw_ref[:, cols], (CONV_ROWS, LANES))
            for j in range(n_taps):
                acc = acc + useq[c, pl.ds(first + j + r0, CONV_ROWS), :] * w_dw_ref[j:j + 1, cols]
            cbuf[r0:r0 + CONV_ROWS, cols] = acc

    glu0 = glu_proj(0)
    glu1 = glu_proj(1)
    glu_to_history(0, glu0)
    qkv = jnp.dot(xb, w_in_ref[:, qkv_cols[0]:qkv_cols[1]], preferred_element_type=F32)
    for c in range(0, n_grp // 2):
        conv_group(c)
    glu_to_history(1, glu1)
    for c in range(n_grp // 2, n_grp):
        conv_group(c)

    scale = HEAD_DIM ** -0.5
    q = [_rope(qkv[:, c * LANES:(c + 1) * LANES], cos, sin) * scale for c in range(ATT_WIDTH // LANES)]
    k = _rope(qkv[:, ATT_WIDTH:ATT_WIDTH + KV_WIDTH], cos, sin)
    v = qkv[:, ATT_WIDTH + KV_WIDTH:ATT_WIDTH + 2 * KV_WIDTH]
    kb, kb_sw = _bf(k), _bf(_swap_halves(k))
    vb2 = jnp.concatenate([_bf(v), _bf(_swap_halves(v))], axis=1)

    half_q = _lane_half((WINDOW, LANES))
    own_block = (lax.broadcasted_iota(jnp.int32, (WINDOW, WINDOW), 1)
                 <= lax.broadcasted_iota(jnp.int32, (WINDOW, WINDOW), 0))
    stack_same = [(c, a) for c in range(4) for a in range(2) if a == c // 2]
    stack_swap = [(c, a) for c in range(4) for a in range(2) if a != c // 2]
    gc_step = (gc_cols[1] - gc_cols[0]) // n_blk

    o_blocks = []
    gate_conv, gate_att = [], []
    for bi in range(n_blk):
        r0 = bi * WINDOW
        if bi == 0:
            kp, kp_sw, vp2 = kprev[...], kprev_sw[...], vprev2[...]
        else:
            kp, kp_sw, vp2 = kb[r0 - WINDOW:r0], kb_sw[r0 - WINDOW:r0], vb2[r0 - WINDOW:r0]
        kk = jnp.concatenate([kp, kb[r0:r0 + WINDOW]], axis=0)
        kk_sw = jnp.concatenate([kp_sw, kb_sw[r0:r0 + WINDOW]], axis=0)
        vv2 = jnp.concatenate([vp2, vb2[r0:r0 + WINDOW]], axis=0)
        scores = []
        for heads, keys in ((stack_same, kk), (stack_swap, kk_sw)):
            qm = jnp.concatenate(
                [_bf(jnp.where(half_q == a, q[c][r0:r0 + WINDOW], 0.0)) for c, a in heads], axis=0)
            scores.append(_mm_t(qm, keys))
        head_out = {}
        for heads, s_all in ((stack_same, scores[0]), (stack_swap, scores[1])):
            probs, inv = [], []
            for n, (c, a) in enumerate(heads):
                s_prev = s_all[n * WINDOW:(n + 1) * WINDOW, 0:WINDOW]
                if bi == 0:
                    s_prev = jnp.where(i == 0, NEG, s_prev)
                s = jnp.where(own_block, s_all[n * WINDOW:(n + 1) * WINDOW, WINDOW:2 * WINDOW], s_prev)
                e, r = _softmax_with_sink(s, sinks_ref[2 * c + a])
                probs.append(_bf(jnp.concatenate([jnp.where(own_block, 0.0, e), jnp.where(own_block, e, 0.0)], axis=1)))
                inv.append(r)
            gates, first_col = (gate_conv, gc_cols[0]) if heads is stack_same else (gate_att, ga_cols[0])
            lo_col = first_col + bi * gc_step
            gates.append(jnp.dot(xb, w_in_ref[:, lo_col:lo_col + gc_step], preferred_element_type=F32))
            pv = jnp.dot(jnp.concatenate(probs, axis=0), vv2, preferred_element_type=F32)
            for n, (c, a) in enumerate(heads):
                head_out[(c, a)] = (pv[n * WINDOW:(n + 1) * WINDOW], inv[n])
        chunks = []
        for c in range(4):
            h = c // 2
            (pv_lo, inv_lo), (pv_hi, inv_hi) = head_out[(c, 0)], head_out[(c, 1)]
            lo = pv_lo[:, (0 if h == 0 else LANES):(LANES if h == 0 else 2 * LANES)] * inv_lo
            hi = pv_hi[:, (0 if h == 1 else LANES):(LANES if h == 1 else 2 * LANES)] * inv_hi
            chunks.append(jnp.where(half_q == 0, lo, hi))
        o_blocks.append(jnp.concatenate(chunks, axis=1))
    o = jnp.concatenate(o_blocks, axis=0)
    att = _mm(o, w_att_ref[...])

    kprev[...] = kb[TM - WINDOW:]
    kprev_sw[...] = kb_sw[TM - WINDOW:]
    vprev2[...] = vb2[TM - WINDOW:]
    kwin_ref[...] = k[TM - WINDOW:]
    vwin_ref[...] = v[TM - WINDOW:]
    for c in range(n_grp):
        cst_ref[:, c * LANES:(c + 1) * LANES] = useq[c, pl.ds(TM + first, n_taps - 1), :]

    cn = _layer_norm(cbuf[...], cg_ref[...], cb_ref[...])
    cproj = _mm(cn * _sigmoid(cn), w_co_ref[...])

    gate_conv = jnp.concatenate(gate_conv, axis=1)
    gate_att = jnp.concatenate(gate_att, axis=1)
    row_groups = [slice(0, TM // 2), slice(TM // 2, TM)]
    mixed = []
    for rows in row_groups:
        merged2 = _one_plus_tanh(gate_att[rows]) * att[rows] + _one_plus_tanh(gate_conv[rows]) * cproj[rows]
        mixed.append(_mm(merged2, w_out_ref[...]))
    for n, rows in enumerate(row_groups):
        x1_ref[rows, :] = _layer_norm(alpha * x[rows] + 0.5 * mixed[n], g1_ref[...], b1_ref[...])


def _prompt_ffn_kernel(x1_ref, p_ref, w_up_ref, w_fc_ref, b_fc_ref, w_dn_ref, g2_ref, b2_ref,
                       wpg_ref, wpp_ref, g3_ref, b3_ref,
                       y_ref, fst_ref, upseq, *, alpha):
    i = pl.program_id(0)
    n_slab = upseq.shape[0]
    tail = upseq.shape[1] - TM
    d_ff = w_dn_ref.shape[0]
    n_taps = w_fc_ref.shape[0]

    @pl.when(i == 0)
    def _():
        upseq[:, TM:TM + tail, :] = jnp.zeros((n_slab, tail, LANES), F32)

    upseq[:, 0:tail, :] = upseq[:, TM:TM + tail, :]

    x1 = x1_ref[...]
    xb = _bf(x1)

    def up_proj(c0):
        return [jnp.dot(xb, w_up_ref[:, base:base + FF_CHUNK], preferred_element_type=F32) for base in (c0, d_ff + c0)]

    def conv_act(c0, ups):
        halves = []
        for base, up in zip((c0, d_ff + c0), ups):
            parts = []
            for s0 in range(0, FF_CHUNK, LANES):
                slab = (base + s0) // LANES
                cols = slice(base + s0, base + s0 + LANES)
                cur = up[:, s0:s0 + LANES]
                upseq[slab, tail:tail + TM, :] = cur
                hcv = b_fc_ref[:, cols] + cur * w_fc_ref[n_taps - 1:n_taps, cols]
                for j in range(n_taps - 1):
                    hcv = hcv + upseq[slab, pl.ds(tail - (n_taps - 1) + j, TM), :] * w_fc_ref[j:j + 1, cols]
                parts.append(hcv)
            halves.append(jnp.concatenate(parts, axis=1))
        return _bf(_gelu(halves[0]) * halves[1])

    starts = list(range(0, d_ff, FF_CHUNK))
    ups = up_proj(starts[0])
    f = None
    e_proj = None
    acts = []
    for n, c0 in enumerate(starts):
        if n + 1 < len(starts):
            nxt = up_proj(starts[n + 1])
        else:
            nxt = None
            e_proj = _mm(p_ref[...], wpp_ref[...])
        acts.append(conv_act(c0, ups))
        if len(acts) == DOWN_GROUP or nxt is None:
            g0 = c0 + FF_CHUNK - len(acts) * FF_CHUNK
            part = jnp.dot(jnp.concatenate(acts, axis=1), w_dn_ref[g0:c0 + FF_CHUNK, :], preferred_element_type=F32)
            f = part if f is None else f + part
            acts = []
        ups = nxt

    for s in range(n_slab):
        fst_ref[:, s * LANES:(s + 1) * LANES] = upseq[s, pl.ds(TM + tail - (n_taps - 1), n_taps - 1), :]

    row_groups = [slice(g * (TM // 4), (g + 1) * (TM // 4)) for g in range(4)]
    ys = _ffn_tail_rows(x1, f, e_proj, g2_ref[...], b2_ref[...], wpg_ref[...], g3_ref[...], b3_ref[...], alpha, row_groups)
    for rows, y in zip(row_groups, ys):
        y_ref[rows, :] = y


def _rows_time_major(ref):
    return jnp.concatenate([ref[:, t, :] for t in range(ref.shape[1])], axis=0)


def _sample_proj_kernel(rope_ref, x_ref, st_ref, w_in_ref, w_dw_ref, b_dw_ref, cg_ref, cb_ref, w_co_ref,
                        q_ref, k_ref, v_ref, cst_ref, ga_ref, ct_ref):
    nb, steps, d_model = x_ref.shape
    n_taps, c_conv = w_dw_ref.shape
    n_hist = n_taps - 1
    qkv_cols, glu_cols, ga_cols, gc_cols = _in_proj_columns(d_model, c_conv)
    x = _rows_time_major(x_ref)
    xb = _bf(x)

    pos = (PAST_LEN + lax.broadcasted_iota(jnp.int32, (steps, LANES), 0)).astype(F32)
    ang = pos * rope_ref[0:1, :]
    cos8 = jnp.cos(ang)
    sin8 = jnp.where(_first_half_of_head((steps, LANES)), -jnp.sin(ang), jnp.sin(ang))
    cos = jnp.concatenate([jnp.broadcast_to(cos8[t:t + 1], (nb, LANES)) for t in range(steps)], axis=0)
    sin = jnp.concatenate([jnp.broadcast_to(sin8[t:t + 1], (nb, LANES)) for t in range(steps)], axis=0)

    qkv = jnp.dot(xb, w_in_ref[:, qkv_cols[0]:qkv_cols[1]], preferred_element_type=F32)
    scale = HEAD_DIM ** -0.5
    q = jnp.concatenate([_rope(qkv[:, c * LANES:(c + 1) * LANES], cos, sin) * scale
                         for c in range(ATT_WIDTH // LANES)], axis=1)
    k = _rope(qkv[:, ATT_WIDTH:ATT_WIDTH + KV_WIDTH], cos, sin)
    v = qkv[:, ATT_WIDTH + KV_WIDTH:ATT_WIDTH + 2 * KV_WIDTH]
    for t in range(steps):
        rows = slice(t * nb, (t + 1) * nb)
        q_ref[:, t, :] = q[rows]
        k_ref[:, t, :] = k[rows]
        v_ref[:, t, :] = v[rows]

    glu = jnp.dot(xb, w_in_ref[:, glu_cols[0]:glu_cols[1]], preferred_element_type=F32)
    u = glu[:, 0:c_conv] * _one_plus_tanh(glu[:, c_conv:2 * c_conv])
    seq = [st_ref[r] for r in range(n_hist)]
    seq += [u[t * nb:(t + 1) * nb] for t in range(steps)]
    outs = []
    for t in range(steps):
        acc = jnp.broadcast_to(b_dw_ref[...], (nb, c_conv))
        for j in range(n_taps):
            acc = acc + seq[t + j] * w_dw_ref[j:j + 1, :]
        outs.append(acc)
    for r in range(n_hist):
        cst_ref[r] = seq[steps + r]
    cn = _layer_norm(jnp.concatenate(outs, axis=0), cg_ref[...], cb_ref[...])
    cproj = _mm(cn * _sigmoid(cn), w_co_ref[...])

    gates = jnp.dot(xb, w_in_ref[:, ga_cols[0]:gc_cols[1]], preferred_element_type=F32)
    ga_ref[...] = 0.5 * _one_plus_tanh(gates[:, 0:d_model])
    ct_ref[...] = 0.5 * _one_plus_tanh(gates[:, d_model:2 * d_model]) * cproj


def _sample_attn_kernel(q_ref, kn_ref, vn_ref, ckt_ref, cvt_ref, sink_ref, o_ref, kwint_ref, vwint_ref, *, steps):
    n_seq = ckt_ref.shape[0]
    wc = ckt_ref.shape[2]
    half8 = _lane_half((steps, LANES))
    n_rows = N_HEADS * steps
    row_t = lax.broadcasted_iota(jnp.int32, (n_rows, 2 * wc), 0) % steps
    col = lax.broadcasted_iota(jnp.int32, (n_rows, 2 * wc), 1)
    new_t = col - (2 * wc - steps)
    valid = jnp.where(col < wc, col - row_t - (wc - WINDOW) - 1, jnp.minimum(new_t, row_t - new_t)) >= 0
    sink = sink_ref[:, 0:1]
    lane_w = lax.broadcasted_iota(jnp.int32, (LANES, wc), 1)
    keep_old = lane_w < (wc - steps)
    pad = jnp.zeros((wc - steps, LANES), F32)

    def one(s):
        rows = pl.ds(pl.multiple_of(s * steps, steps), steps)
        qs = q_ref[rows, :]
        blocks = []
        for c in range(ATT_WIDTH // LANES):
            qc = qs[:, c * LANES:(c + 1) * LANES]
            qc_sw = _swap_halves(qc)
            h = c // 2
            for a in range(2):
                blocks.append(jnp.where(half8 == h, qc if a == h else qc_sw, 0.0))
        lhs = _bf(jnp.concatenate(blocks, axis=0))
        kct, vct = ckt_ref[s], cvt_ref[s]
        knt = jnp.concatenate([pad, kn_ref[rows, :]], axis=0).T
        vnt = jnp.concatenate([pad, vn_ref[rows, :]], axis=0).T
        sc = jnp.dot(lhs, _bf(jnp.concatenate([kct, knt], axis=1)), preferred_element_type=F32)
        e, r = _softmax_with_sink(jnp.where(valid, sc, NEG), sink)
        out = _mm_t(_bf(e), _bf(jnp.concatenate([vct, vnt], axis=1))) * r
        chunks = []
        for c in range(ATT_WIDTH // LANES):
            h = c // 2
            lo = out[(2 * c) * steps:(2 * c + 1) * steps]
            hi = out[(2 * c + 1) * steps:(2 * c + 2) * steps]
            lo = lo if h == 0 else _swap_halves(lo)
            hi = hi if h == 1 else _swap_halves(hi)
            chunks.append(jnp.where(half8 == 0, lo, hi))
        o_ref[rows, :] = jnp.concatenate(chunks, axis=1)
        kwint_ref[s] = jnp.where(keep_old, pltpu.roll(kct, wc - steps, 1), knt)
        vwint_ref[s] = jnp.where(keep_old, pltpu.roll(vct, wc - steps, 1), vnt)

    def group(g, carry):
        for n in range(ATT_UNROLL):
            one(g * ATT_UNROLL + n)
        return carry

    lax.fori_loop(0, n_seq // ATT_UNROLL, group, 0)


def _sample_ffn_kernel(o_ref, ga_ref, ct_ref, x_ref, p_ref, st_ref, w_att_ref, w_out_ref, g1_ref, b1_ref,
                       w_up_ref, w_fc_ref, b_fc_ref, w_dn_ref, g2_ref, b2_ref, wpg_ref, wpp_ref, g3_ref, b3_ref,
                       y_ref, fst_ref, *, alpha):
    nb, steps, d_model = x_ref.shape
    d_ff = w_dn_ref.shape[0]
    n_taps = w_fc_ref.shape[0]
    n_hist = n_taps - 1

    o = _rows_time_major(o_ref)
    x = _rows_time_major(x_ref)
    p = _rows_time_major(p_ref)
    att = _mm(o, w_att_ref[...])
    mixed = _mm(ga_ref[...] * att + ct_ref[...], w_out_ref[...])
    x1 = _layer_norm(alpha * x + mixed, g1_ref[...], b1_ref[...])
    xb = _bf(x1)

    def up_proj(c0):
        return [jnp.dot(xb, w_up_ref[:, base:base + FF_CHUNK], preferred_element_type=F32) for base in (c0, d_ff + c0)]

    def conv_act(c0, ups):
        halves = []
        for base, up in zip((c0, d_ff + c0), ups):
            cols = slice(base, base + FF_CHUNK)
            seq = [st_ref[:, r, cols] for r in range(n_hist)]
            seq += [up[t * nb:(t + 1) * nb] for t in range(steps)]
            outs = []
            for t in range(steps):
                acc = jnp.broadcast_to(b_fc_ref[:, cols], (nb, FF_CHUNK))
                for j in range(n_taps):
                    acc = acc + seq[t + j] * w_fc_ref[j:j + 1, cols]
                outs.append(acc)
            for r in range(n_hist):
                fst_ref[:, r, cols] = seq[steps + r]
            halves.append(jnp.concatenate(outs, axis=0))
        return _bf(_gelu(halves[0]) * halves[1])

    starts = list(range(0, d_ff, FF_CHUNK))
    ups = up_proj(starts[0])
    f = None
    acts = []
    for n, c0 in enumerate(starts):
        nxt = up_proj(starts[n + 1]) if n + 1 < len(starts) else None
        acts.append(conv_act(c0, ups))
        if len(acts) == DOWN_GROUP or nxt is None:
            g0 = c0 + FF_CHUNK - len(acts) * FF_CHUNK
            part = jnp.dot(jnp.concatenate(acts, axis=1), w_dn_ref[g0:c0 + FF_CHUNK, :], preferred_element_type=F32)
            f = part if f is None else f + part
            acts = []
        ups = nxt

    e_proj = _mm(p, wpp_ref[...])
    y, = _ffn_tail_rows(x1, f, e_proj, g2_ref[...], b2_ref[...], wpg_ref[...], g3_ref[...], b3_ref[...], alpha,
                        [slice(0, steps * nb)])
    for t in range(steps):
        y_ref[:, t, :] = y[t * nb:(t + 1) * nb]


def _const_spec(shape):
    zeros = (0,) * len(shape)
    return pl.BlockSpec(shape, lambda i: zeros, pipeline_mode=pl.Buffered(1))


def _const_out_spec(shape):
    zeros = (0,) * len(shape)
    return pl.BlockSpec(shape, lambda i: zeros)


def _row_spec(rows, width):
    return pl.BlockSpec((rows, width), lambda i: (i, 0))


def _params():
    return pltpu.CompilerParams(dimension_semantics=("arbitrary",), vmem_limit_bytes=VMEM_LIMIT)


def _rope_rows():
    half = HEAD_DIM // 2
    inv = ROPE_THETA ** (-jnp.arange(half, dtype=F32) / half)
    return jnp.broadcast_to(jnp.tile(inv, LANES // half)[None, :], (8, LANES))


def _row2d(v):
    return v.reshape(1, -1)


def _slice_spec(rows, cols, n_steps):
    for held in (1, 2, 4, 8):
        if (rows * held) % n_steps == 0 and (rows * held // n_steps) % 16 == 0:
            return pl.BlockSpec((rows * held // n_steps, cols), lambda i, held=held: (i // held, 0))
    raise ValueError(f"cannot slice {rows} rows over {n_steps} steps")


def _prompt_layer(x, p, rope_rows, w, late_f32, alpha):
    t_len, d_model = x.shape
    n_taps, c_conv = w["w_dw"].shape
    d_ff = late_f32["w_down"].shape[0]
    f_taps = w["w_fconv"].shape[0]
    conv_tail = -(-(n_taps - 1) // 8) * 8
    ffn_tail = -(-(f_taps - 1) // 8) * 8
    grid = (t_len // TM,)
    late_names = list(late_f32)
    late_specs = [_slice_spec(*late_f32[k].shape, grid[0]) for k in late_names]

    mixer_in = [w["sinks"], rope_rows, x, w["w_in"], w["w_attn_out"], w["w_dw"], _row2d(w["b_dw"]),
                _row2d(w["conv_ln_g"]), _row2d(w["conv_ln_b"]), w["w_conv_out"], w["w_out"],
                _row2d(w["ln1_g"]), _row2d(w["ln1_b"])]
    mixer_specs = [pl.BlockSpec(memory_space=pltpu.SMEM), _const_spec(rope_rows.shape), _row_spec(TM, d_model)]
    mixer_specs += [_const_spec(a.shape) for a in mixer_in[3:]]
    x1, kwin, vwin, cst, *late_bf16 = pl.pallas_call(
        functools.partial(_prompt_mixer_kernel, alpha=alpha, n_late=len(late_names)),
        grid=grid,
        in_specs=mixer_specs + late_specs,
        out_specs=[_row_spec(TM, d_model), _const_out_spec((WINDOW, LANES)), _const_out_spec((WINDOW, LANES)),
                   _const_out_spec((n_taps - 1, c_conv))] + late_specs,
        out_shape=[jax.ShapeDtypeStruct((t_len, d_model), F32), jax.ShapeDtypeStruct((WINDOW, LANES), F32),
                   jax.ShapeDtypeStruct((WINDOW, LANES), F32), jax.ShapeDtypeStruct((n_taps - 1, c_conv), F32)]
                  + [jax.ShapeDtypeStruct(late_f32[k].shape, BF16) for k in late_names],
        scratch_shapes=[pltpu.VMEM((WINDOW, LANES), BF16), pltpu.VMEM((WINDOW, LANES), BF16),
                        pltpu.VMEM((WINDOW, 2 * LANES), BF16),
                        pltpu.VMEM((c_conv // LANES, TM + conv_tail, LANES), F32),
                        pltpu.VMEM((TM, c_conv), F32),
                        pltpu.VMEM((2, TM, LANES), F32)],
        compiler_params=_params(),
        name="prompt_mixer",
    )(*mixer_in, *[late_f32[k] for k in late_names])
    w = dict(w, **dict(zip(late_names, late_bf16)))

    ffn_in = [x1, p, w["w_up"], w["w_fconv"], _row2d(w["b_fconv"]), w["w_down"], _row2d(w["ln2_g"]), _row2d(w["ln2_b"]),
              w["w_ple_gate"], w["w_ple_proj"], _row2d(w["ln3_g"]), _row2d(w["ln3_b"])]
    ffn_specs = [_row_spec(TM, d_model), _row_spec(TM, p.shape[1])] + [_const_spec(a.shape) for a in ffn_in[2:]]
    y, fst = pl.pallas_call(
        functools.partial(_prompt_ffn_kernel, alpha=alpha),
        grid=grid,
        in_specs=ffn_specs,
        out_specs=[_row_spec(TM, d_model), _const_out_spec((f_taps - 1, 2 * d_ff))],
        out_shape=[jax.ShapeDtypeStruct((t_len, d_model), F32), jax.ShapeDtypeStruct((f_taps - 1, 2 * d_ff), F32)],
        scratch_shapes=[pltpu.VMEM((2 * d_ff // LANES, TM + ffn_tail, LANES), F32)],
        compiler_params=_params(),
        name="prompt_ffn",
    )(*ffn_in)
    return y, kwin, vwin, cst, fst, dict(zip(late_names, late_bf16))


def _sample_layer(x, p, cache_kt, cache_vt, st_conv, st_ffn, rope_rows, w, alpha):
    n_seq, steps, d_model = x.shape
    wc = cache_kt.shape[2]
    n_tok = n_seq * steps
    blk_rows = SEQ_BLK * steps
    grid = (n_seq // SEQ_BLK,)
    seq_spec = lambda mid, width: pl.BlockSpec((SEQ_BLK, mid, width), lambda i: (i, 0, 0))
    hist_spec = pl.BlockSpec((st_conv.shape[0], SEQ_BLK, st_conv.shape[2]), lambda i: (0, i, 0))

    proj_in = [rope_rows, x, st_conv, w["w_in"], w["w_dw"], _row2d(w["b_dw"]), _row2d(w["conv_ln_g"]),
               _row2d(w["conv_ln_b"]), w["w_conv_out"]]
    proj_specs = [_const_spec(rope_rows.shape), seq_spec(steps, d_model), hist_spec]
    proj_specs += [_const_spec(a.shape) for a in proj_in[3:]]
    q, kn, vn, cst, ga, ct = pl.pallas_call(
        _sample_proj_kernel,
        grid=grid,
        in_specs=proj_specs,
        out_specs=[seq_spec(steps, ATT_WIDTH), seq_spec(steps, KV_WIDTH), seq_spec(steps, KV_WIDTH), hist_spec,
                   _row_spec(blk_rows, d_model), _row_spec(blk_rows, d_model)],
        out_shape=[jax.ShapeDtypeStruct((n_seq, steps, ATT_WIDTH), F32), jax.ShapeDtypeStruct((n_seq, steps, KV_WIDTH), F32),
                   jax.ShapeDtypeStruct((n_seq, steps, KV_WIDTH), F32), jax.ShapeDtypeStruct(st_conv.shape, F32),
                   jax.ShapeDtypeStruct((n_tok, d_model), F32), jax.ShapeDtypeStruct((n_tok, d_model), F32)],
        compiler_params=_params(),
        name="sample_proj",
    )(*proj_in)

    sink_rows = jnp.broadcast_to(jnp.repeat(w["sinks"], steps)[:, None], (N_HEADS * steps, LANES))
    att_rows = ATT_BLK * steps
    cache_spec = pl.BlockSpec((ATT_BLK, KV_WIDTH, wc), lambda i: (i, 0, 0))
    o, kwin_t, vwin_t = pl.pallas_call(
        functools.partial(_sample_attn_kernel, steps=steps),
        grid=(n_seq // ATT_BLK,),
        in_specs=[_row_spec(att_rows, ATT_WIDTH), _row_spec(att_rows, KV_WIDTH), _row_spec(att_rows, KV_WIDTH),
                  cache_spec, cache_spec, _const_spec(sink_rows.shape)],
        out_specs=[_row_spec(att_rows, ATT_WIDTH), cache_spec, cache_spec],
        out_shape=[jax.ShapeDtypeStruct((n_tok, ATT_WIDTH), F32), jax.ShapeDtypeStruct(cache_kt.shape, F32),
                   jax.ShapeDtypeStruct(cache_vt.shape, F32)],
        compiler_params=_params(),
        name="sample_attn",
    )(q.reshape(n_tok, ATT_WIDTH), kn.reshape(n_tok, KV_WIDTH), vn.reshape(n_tok, KV_WIDTH), cache_kt, cache_vt, sink_rows)

    ffn_in = [o.reshape(n_seq, steps, ATT_WIDTH), ga, ct, x, p, st_ffn, w["w_attn_out"], w["w_out"], _row2d(w["ln1_g"]),
              _row2d(w["ln1_b"]), w["w_up"], w["w_fconv"], _row2d(w["b_fconv"]), w["w_down"], _row2d(w["ln2_g"]),
              _row2d(w["ln2_b"]), w["w_ple_gate"], w["w_ple_proj"], _row2d(w["ln3_g"]), _row2d(w["ln3_b"])]
    ffn_specs = [seq_spec(steps, ATT_WIDTH), _row_spec(blk_rows, d_model), _row_spec(blk_rows, d_model),
                 seq_spec(steps, d_model), seq_spec(steps, p.shape[2]), seq_spec(*st_ffn.shape[1:])]
    ffn_specs += [_const_spec(a.shape) for a in ffn_in[6:]]
    y, fst = pl.pallas_call(
        functools.partial(_sample_ffn_kernel, alpha=alpha),
        grid=grid,
        in_specs=ffn_specs,
        out_specs=[seq_spec(steps, d_model), seq_spec(*st_ffn.shape[1:])],
        out_shape=[jax.ShapeDtypeStruct(x.shape, F32), jax.ShapeDtypeStruct(st_ffn.shape, F32)],
        compiler_params=_params(),
        name="sample_ffn",
    )(*ffn_in)
    return y, kwin_t, vwin_t, cst, fst


def kernel(x_prompt, x_sample, cache_k, cache_v, state_conv, state_ffn_conv, p_prompt, p_sample, w_in, sinks, w_attn_out, w_dw, b_dw, conv_ln_g, conv_ln_b, w_conv_out, w_out, ln1_g, ln1_b, w_up, w_fconv, b_fconv, w_down, ln2_g, ln2_b, w_ple_gate, w_ple_proj, ln3_g, ln3_b):
    depth = w_in.shape[0]
    bp, t_len, d_model = x_prompt.shape
    n_seq, steps, _ = x_sample.shape
    wc = cache_k.shape[2]
    assert bp == 1 and t_len % TM == 0 and n_seq % SEQ_BLK == 0 and n_seq % ATT_BLK == 0 and ATT_BLK % ATT_UNROLL == 0
    assert KV_WIDTH == LANES and wc == WINDOW and wc == LANES and steps == 8
    assert cache_k.shape[3:] == (N_KV_HEADS, HEAD_DIM)
    alpha = (2 * depth) ** 0.25
    rope_rows = _rope_rows()
    c_conv = w_dw.shape[2]
    qkv_cols = _in_proj_columns(d_model, c_conv)[0]
    gated_col_scale = jnp.where(jnp.arange(w_in.shape[2]) < qkv_cols[1], 1.0, 0.5).astype(F32)

    matmul_weights = dict(w_in=w_in, w_attn_out=w_attn_out, w_conv_out=w_conv_out, w_out=w_out)
    late_weights = dict(w_up=w_up, w_down=w_down, w_ple_gate=w_ple_gate, w_ple_proj=w_ple_proj)
    other = dict(sinks=sinks, w_dw=w_dw, b_dw=b_dw, conv_ln_g=conv_ln_g, conv_ln_b=conv_ln_b, ln1_g=ln1_g, ln1_b=ln1_b,
                 w_fconv=w_fconv, b_fconv=b_fconv, ln2_g=ln2_g, ln2_b=ln2_b, ln3_g=ln3_g, ln3_b=ln3_b)

    def keys_on_lanes(c):
        return jnp.transpose(c, (0, 2, 3, 1)).reshape(n_seq, KV_WIDTH, wc)

    def keys_on_rows(ct):
        return jnp.transpose(ct.reshape(n_seq, N_KV_HEADS, HEAD_DIM, wc), (0, 3, 1, 2))

    yp = x_prompt.reshape(t_len, d_model)
    ys = x_sample
    outs = [[] for _ in range(8)]
    for l in range(depth):
        w = {name: _bf(a[l]) for name, a in matmul_weights.items()}
        w["w_in"] = _bf(w_in[l] * gated_col_scale)
        w.update({name: a[l] for name, a in other.items()})
        yp, kp, vp, cp, fp, late_bf16 = _prompt_layer(yp, p_prompt[l, 0], rope_rows, w,
                                                      {name: a[l] for name, a in late_weights.items()}, alpha)
        w.update(late_bf16)
        ys, ks_t, vs_t, cs, fs = _sample_layer(
            ys, p_sample[l], keys_on_lanes(cache_k[l]), keys_on_lanes(cache_v[l]),
            jnp.transpose(state_conv[l], (1, 0, 2)), state_ffn_conv[l], rope_rows, w, alpha)
        kv_shape = (N_KV_HEADS, HEAD_DIM)
        for lst, a in zip(outs, (kp.reshape(1, wc, *kv_shape), vp.reshape(1, wc, *kv_shape), cp[None], fp[None],
                                 keys_on_rows(ks_t), keys_on_rows(vs_t), jnp.transpose(cs, (1, 0, 2)), fs)):
            lst.append(a)
    return (yp.reshape(x_prompt.shape), ys) + tuple(jnp.stack(lst) for lst in outs)
```

```python
import functools
import math

import jax
import jax.numpy as jnp
from jax import lax
from jax.experimental import pallas as pl
from jax.experimental.pallas import tpu as pltpu

F32 = jnp.float32
BF16 = jnp.bfloat16

LANES = 128
HEAD_DIM = 64
N_HEADS = 8
N_KV_HEADS = 2
ATT_WIDTH = N_HEADS * HEAD_DIM
KV_WIDTH = N_KV_HEADS * HEAD_DIM
WINDOW = 128
ROPE_THETA = 10000.0
PAST_LEN = 16384
LN_EPS = 1e-5
NEG = -1e30
GELU_C = math.sqrt(2.0 / math.pi)

TM = 512
SEQ_BLK = 32
ATT_BLK = 16
ATT_UNROLL = 4
FF_CHUNK = 256
DOWN_GROUP = 4
CONV_ROWS = 128
VMEM_LIMIT = 56 * 1024 * 1024


def _in_proj_columns(d_model, c_conv):
    qkv = ATT_WIDTH + 2 * KV_WIDTH
    glu = qkv + 2 * c_conv
    return (0, qkv), (qkv, glu), (glu, glu + d_model), (glu + d_model, glu + 2 * d_model)


def _bf(x):
    return x.astype(BF16)


def _mm(a, w):
    return jnp.dot(_bf(a), w, preferred_element_type=F32)


def _mm_t(a, b):
    return lax.dot_general(a, b, (((1,), (1,)), ((), ())), preferred_element_type=F32)


def _sigmoid(x):
    return 0.5 * (jnp.tanh(0.5 * x) + 1.0)


def _one_plus_tanh(h):
    return jnp.tanh(h) + 1.0


def _gelu(x):
    return 0.5 * x * (1.0 + jnp.tanh(GELU_C * (x + 0.044715 * (x * x * x))))


def _layer_norm(x, g, b):
    mu = jnp.mean(x, axis=-1, keepdims=True)
    d = x - mu
    var = jnp.mean(d * d, axis=-1, keepdims=True)
    return d * lax.rsqrt(var + LN_EPS) * g + b


def _lane_half(shape):
    return (lax.broadcasted_iota(jnp.int32, shape, len(shape) - 1) // HEAD_DIM) % 2


def _first_half_of_head(shape):
    lane = lax.broadcasted_iota(jnp.int32, shape, len(shape) - 1)
    return (lane % HEAD_DIM) < (HEAD_DIM // 2)


def _rope(x, cos, sin_signed):
    partner = jnp.where(_first_half_of_head(x.shape), pltpu.roll(x, LANES - HEAD_DIM // 2, 1),
                        pltpu.roll(x, HEAD_DIM // 2, 1))
    return x * cos + partner * sin_signed


def _swap_halves(x):
    return pltpu.roll(x, HEAD_DIM, 1)


def _softmax_with_sink(s, sink):
    m = jnp.maximum(jnp.max(s, axis=-1, keepdims=True), sink)
    e = jnp.exp(s - m)
    den = jnp.sum(e, axis=-1, keepdims=True) + jnp.exp(sink - m)
    return e, 1.0 / den


def _ffn_tail_rows(x1, f, e_proj, ln2_g, ln2_b, wpg, ln3_g, ln3_b, alpha, row_groups):
    x2 = []
    gate = []
    for rows in row_groups:
        x2.append(_layer_norm(alpha * x1[rows] + f[rows], ln2_g, ln2_b))
        gate.append(_mm(x2[-1], wpg))
    return [_layer_norm(alpha * x2[n] + _sigmoid(gate[n]) * e_proj[rows], ln3_g, ln3_b)
            for n, rows in enumerate(row_groups)]


def _prompt_mixer_kernel(sinks_ref, rope_ref, x_ref, w_in_ref, w_att_ref, w_dw_ref, b_dw_ref,
                         cg_ref, cb_ref, w_co_ref, w_out_ref, g1_ref, b1_ref, *rest, alpha, n_late):
    late_f32, rest = rest[:n_late], rest[n_late:]
    (x1_ref, kwin_ref, vwin_ref, cst_ref), rest = rest[:4], rest[4:]
    late_bf16, (kprev, kprev_sw, vprev2, useq, cbuf, rtab) = rest[:n_late], rest[n_late:]
    i = pl.program_id(0)

    for src, dst in zip(late_f32, late_bf16):
        dst[...] = _bf(src[...])

    tail = useq.shape[1] - TM
    n_grp = useq.shape[0]
    n_taps = w_dw_ref.shape[0]
    first = tail - (n_taps - 1)
    d_model = x_ref.shape[1]
    c_conv = n_grp * LANES
    qkv_cols, glu_cols, ga_cols, gc_cols = _in_proj_columns(d_model, c_conv)
    inv_freq = rope_ref[0:1, :]
    n_blk = TM // WINDOW

    @pl.when(i == 0)
    def _():
        kprev[...] = jnp.zeros_like(kprev)
        kprev_sw[...] = jnp.zeros_like(kprev_sw)
        vprev2[...] = jnp.zeros_like(vprev2)
        useq[:, TM:TM + tail, :] = jnp.zeros((n_grp, tail, LANES), F32)
        ang = lax.broadcasted_iota(jnp.int32, (TM, LANES), 0).astype(F32) * inv_freq
        rtab[0] = jnp.cos(ang)
        rtab[1] = jnp.sin(ang)

    useq[:, 0:tail, :] = useq[:, TM:TM + tail, :]

    x = x_ref[...]
    xb = _bf(x)

    ang_b = (i * TM).astype(F32) * inv_freq
    cos_b, sin_b = jnp.cos(ang_b), jnp.sin(ang_b)
    cos_r, sin_r = rtab[0], rtab[1]
    cos = cos_b * cos_r - sin_b * sin_r
    sin = sin_b * cos_r + cos_b * sin_r
    sin = jnp.where(_first_half_of_head((TM, LANES)), -sin, sin)

    half_c = c_conv // 2

    def glu_proj(hh):
        lo = glu_cols[0] + hh * half_c
        return (jnp.dot(xb, w_in_ref[:, lo:lo + half_c], preferred_element_type=F32),
                jnp.dot(xb, w_in_ref[:, lo + c_conv:lo + c_conv + half_c], preferred_element_type=F32))

    def glu_to_history(hh, ab):
        u = ab[0] * _one_plus_tanh(ab[1])
        for cc in range(half_c // LANES):
            useq[hh * (half_c // LANES) + cc, tail:tail + TM, :] = u[:, cc * LANES:(cc + 1) * LANES]

    def conv_group(c):
        cols = slice(c * LANES, (c + 1) * LANES)
        for r0 in range(0, TM, CONV_ROWS):
            acc = jnp.broadcast_to(b_dw_ref[:, cols], (CONV_ROWS, LANES))
            for j in range(n_taps):
                acc = acc + useq[c, pl.ds(first + j + r0, CONV_ROWS), :] * w_dw_ref[j:j + 1, cols]
            cbuf[r0:r0 + CONV_ROWS, cols] = acc

    glu0 = glu_proj(0)
    glu1 = glu_proj(1)
    glu_to_history(0, glu0)
    qkv = jnp.dot(xb, w_in_ref[:, qkv_cols[0]:qkv_cols[1]], preferred_element_type=F32)
    for c in range(0, n_grp // 2):
        conv_group(c)
    glu_to_history(1, glu1)
    for c in range(n_grp // 2, n_grp):
        conv_group(c)

    scale = HEAD_DIM ** -0.5
    q = [_rope(qkv[:, c * LANES:(c + 1) * LANES], cos, sin) * scale for c in range(ATT_WIDTH // LANES)]
    k = _rope(qkv[:, ATT_WIDTH:ATT_WIDTH + KV_WIDTH], cos, sin)
    v = qkv[:, ATT_WIDTH + KV_WIDTH:ATT_WIDTH + 2 * KV_WIDTH]
    kb, kb_sw = _bf(k), _bf(_swap_halves(k))
    vb2 = jnp.concatenate([_bf(v), _bf(_swap_halves(v))], axis=1)

    half_q = _lane_half((WINDOW, LANES))
    own_block = (lax.broadcasted_iota(jnp.int32, (WINDOW, WINDOW), 1)
                 <= lax.broadcasted_iota(jnp.int32, (WINDOW, WINDOW), 0))
    stack_same = [(c, a) for c in range(4) for a in range(2) if a == c // 2]
    stack_swap = [(c, a) for c in range(4) for a in range(2) if a != c // 2]
    gc_step = (gc_cols[1] - gc_cols[0]) // n_blk

    o_blocks = []
    gate_conv, gate_att = [], []
    for bi in range(n_blk):
        r0 = bi * WINDOW
        if bi == 0:
            kp, kp_sw, vp2 = kprev[...], kprev_sw[...], vprev2[...]
        else:
            kp, kp_sw, vp2 = kb[r0 - WINDOW:r0], kb_sw[r0 - WINDOW:r0], vb2[r0 - WINDOW:r0]
        kk = jnp.concatenate([kp, kb[r0:r0 + WINDOW]], axis=0)
        kk_sw = jnp.concatenate([kp_sw, kb_sw[r0:r0 + WINDOW]], axis=0)
        vv2 = jnp.concatenate([vp2, vb2[r0:r0 + WINDOW]], axis=0)
        scores = []
        for heads, keys in ((stack_same, kk), (stack_swap, kk_sw)):
            qm = jnp.concatenate(
                [_bf(jnp.where(half_q == a, q[c][r0:r0 + WINDOW], 0.0)) for c, a in heads], axis=0)
            scores.append(_mm_t(qm, keys))
        head_out = {}
        for heads, s_all in ((stack_same, scores[0]), (stack_swap, scores[1])):
            probs, inv = [], []
            for n, (c, a) in enumerate(heads):
                s_prev = s_all[n * WINDOW:(n + 1) * WINDOW, 0:WINDOW]
                if bi == 0:
                    s_prev = jnp.where(i == 0, NEG, s_prev)
                s = jnp.where(own_block, s_all[n * WINDOW:(n + 1) * WINDOW, WINDOW:2 * WINDOW], s_prev)
                e, r = _softmax_with_sink(s, sinks_ref[2 * c + a])
                probs.append(_bf(jnp.concatenate([jnp.where(own_block, 0.0, e), jnp.where(own_block, e, 0.0)], axis=1)))
                inv.append(r)
            gates, first_col = (gate_conv, gc_cols[0]) if heads is stack_same else (gate_att, ga_cols[0])
            lo_col = first_col + bi * gc_step
            gates.append(jnp.dot(xb, w_in_ref[:, lo_col:lo_col + gc_step], preferred_element_type=F32))
            pv = jnp.dot(jnp.concatenate(probs, axis=0), vv2, preferred_element_type=F32)
            for n, (c, a) in enumerate(heads):
                head_out[(c, a)] = (pv[n * WINDOW:(n + 1) * WINDOW], inv[n])
        chunks = []
        for c in range(4):
            h = c // 2
            (pv_lo, inv_lo), (pv_hi, inv_hi) = head_out[(c, 0)], head_out[(c, 1)]
            lo = pv_lo[:, (0 if h == 0 else LANES):(LANES if h == 0 else 2 * LANES)] * inv_lo
            hi = pv_hi[:, (0 if h == 1 else LANES):(LANES if h == 1 else 2 * LANES)] * inv_hi
            chunks.append(jnp.where(half_q == 0, lo, hi))
        o_blocks.append(jnp.concatenate(chunks, axis=1))
    o = jnp.concatenate(o_blocks, axis=0)
    att = _mm(o, w_att_ref[...])

    kprev[...] = kb[TM - WINDOW:]
    kprev_sw[...] = kb_sw[TM - WINDOW:]
    vprev2[...] = vb2[TM - WINDOW:]
    kwin_ref[...] = k[TM - WINDOW:]
    vwin_ref[...] = v[TM - WINDOW:]
    for c in range(n_grp):
        cst_ref[:, c * LANES:(c + 1) * LANES] = useq[c, pl.ds(TM + first, n_taps - 1), :]

    cn = _layer_norm(cbuf[...], cg_ref[...], cb_ref[...])
    cproj = _mm(cn * _sigmoid(cn), w_co_ref[...])

    gate_conv = jnp.concatenate(gate_conv, axis=1)
    gate_att = jnp.concatenate(gate_att, axis=1)
    row_groups = [slice(0, TM // 2), slice(TM // 2, TM)]
    mixed = []
    for rows in row_groups:
        merged2 = _one_plus_tanh(gate_att[rows]) * att[rows] + _one_plus_tanh(gate_conv[rows]) * cproj[rows]
        mixed.append(_mm(merged2, w_out_ref[...]))
    for n, rows in enumerate(row_groups):
        x1_ref[rows, :] = _layer_norm(alpha * x[rows] + 0.5 * mixed[n], g1_ref[...], b1_ref[...])


def _prompt_ffn_kernel(x1_ref, p_ref, w_up_ref, w_fc_ref, b_fc_ref, w_dn_ref, g2_ref, b2_ref,
                       wpg_ref, wpp_ref, g3_ref, b3_ref,
                       y_ref, fst_ref, upseq, *, alpha):
    i = pl.program_id(0)
    n_slab = upseq.shape[0]
    tail = upseq.shape[1] - TM
    d_ff = w_dn_ref.shape[0]
    n_taps = w_fc_ref.shape[0]

    @pl.when(i == 0)
    def _():
        upseq[:, TM:TM + tail, :] = jnp.zeros((n_slab, tail, LANES), F32)

    upseq[:, 0:tail, :] = upseq[:, TM:TM + tail, :]

    x1 = x1_ref[...]
    xb = _bf(x1)

    def up_proj(c0):
        return [jnp.dot(xb, w_up_ref[:, base:base + FF_CHUNK], preferred_element_type=F32) for base in (c0, d_ff + c0)]

    def conv_act(c0, ups):
        halves = []
        for base, up in zip((c0, d_ff + c0), ups):
            parts = []
            for s0 in range(0, FF_CHUNK, LANES):
                slab = (base + s0) // LANES
                cols = slice(base + s0, base + s0 + LANES)
                cur = up[:, s0:s0 + LANES]
                upseq[slab, tail:tail + TM, :] = cur
                hcv = b_fc_ref[:, cols] + cur * w_fc_ref[n_taps - 1:n_taps, cols]
                for j in range(n_taps - 1):
                    hcv = hcv + upseq[slab, pl.ds(tail - (n_taps - 1) + j, TM), :] * w_fc_ref[j:j + 1, cols]
                parts.append(hcv)
            halves.append(jnp.concatenate(parts, axis=1))
        return _bf(_gelu(halves[0]) * halves[1])

    starts = list(range(0, d_ff, FF_CHUNK))
    ups = up_proj(starts[0])
    f = None
    e_proj = None
    acts = []
    for n, c0 in enumerate(starts):
        if n + 1 < len(starts):
            nxt = up_proj(starts[n + 1])
        else:
            nxt = None
            e_proj = _mm(p_ref[...], wpp_ref[...])
        acts.append(conv_act(c0, ups))
        if len(acts) == DOWN_GROUP or nxt is None:
            g0 = c0 + FF_CHUNK - len(acts) * FF_CHUNK
            part = jnp.dot(jnp.concatenate(acts, axis=1), w_dn_ref[g0:c0 + FF_CHUNK, :], preferred_element_type=F32)
            f = part if f is None else f + part
            acts = []
        ups = nxt

    for s in range(n_slab):
        fst_ref[:, s * LANES:(s + 1) * LANES] = upseq[s, pl.ds(TM + tail - (n_taps - 1), n_taps - 1), :]

    row_groups = [slice(g * (TM // 4), (g + 1) * (TM // 4)) for g in range(4)]
    ys = _ffn_tail_rows(x1, f, e_proj, g2_ref[...], b2_ref[...], wpg_ref[...], g3_ref[...], b3_ref[...], alpha, row_groups)
    for rows, y in zip(row_groups, ys):
        y_ref[rows, :] = y


def _rows_time_major(ref):
    return jnp.concatenate([ref[:, t, :] for t in range(ref.shape[1])], axis=0)


def _sample_proj_kernel(rope_ref, x_ref, st_ref, w_in_ref, w_dw_ref, b_dw_ref, cg_ref, cb_ref, w_co_ref,
                        q_ref, k_ref, v_ref, cst_ref, ga_ref, ct_ref):
    nb, steps, d_model = x_ref.shape
    n_taps, c_conv = w_dw_ref.shape
    n_hist = n_taps - 1
    qkv_cols, glu_cols, ga_cols, gc_cols = _in_proj_columns(d_model, c_conv)
    x = _rows_time_major(x_ref)
    xb = _bf(x)

    pos = (PAST_LEN + lax.broadcasted_iota(jnp.int32, (steps, LANES), 0)).astype(F32)
    ang = pos * rope_ref[0:1, :]
    cos8 = jnp.cos(ang)
    sin8 = jnp.where(_first_half_of_head((steps, LANES)), -jnp.sin(ang), jnp.sin(ang))
    cos = jnp.concatenate([jnp.broadcast_to(cos8[t:t + 1], (nb, LANES)) for t in range(steps)], axis=0)
    sin = jnp.concatenate([jnp.broadcast_to(sin8[t:t + 1], (nb, LANES)) for t in range(steps)], axis=0)

    qkv = jnp.dot(xb, w_in_ref[:, qkv_cols[0]:qkv_cols[1]], preferred_element_type=F32)
    scale = HEAD_DIM ** -0.5
    q = jnp.concatenate([_rope(qkv[:, c * LANES:(c + 1) * LANES], cos, sin) * scale
                         for c in range(ATT_WIDTH // LANES)], axis=1)
    k = _rope(qkv[:, ATT_WIDTH:ATT_WIDTH + KV_WIDTH], cos, sin)
    v = qkv[:, ATT_WIDTH + KV_WIDTH:ATT_WIDTH + 2 * KV_WIDTH]
    for t in range(steps):
        rows = slice(t * nb, (t + 1) * nb)
        q_ref[:, t, :] = q[rows]
        k_ref[:, t, :] = k[rows]
        v_ref[:, t, :] = v[rows]

    glu = jnp.dot(xb, w_in_ref[:, glu_cols[0]:glu_cols[1]], preferred_element_type=F32)
    u = glu[:, 0:c_conv] * _one_plus_tanh(glu[:, c_conv:2 * c_conv])
    seq = [st_ref[r] for r in range(n_hist)]
    seq += [u[t * nb:(t + 1) * nb] for t in range(steps)]
    outs = []
    for t in range(steps):
        acc = jnp.broadcast_to(b_dw_ref[...], (nb, c_conv))
        for j in range(n_taps):
            acc = acc + seq[t + j] * w_dw_ref[j:j + 1, :]
        outs.append(acc)
    for r in range(n_hist):
        cst_ref[r] = seq[steps + r]
    cn = _layer_norm(jnp.concatenate(outs, axis=0), cg_ref[...], cb_ref[...])
    cproj = _mm(cn * _sigmoid(cn), w_co_ref[...])

    gates = jnp.dot(xb, w_in_ref[:, ga_cols[0]:gc_cols[1]], preferred_element_type=F32)
    ga_ref[...] = 0.5 * _one_plus_tanh(gates[:, 0:d_model])
    ct_ref[...] = 0.5 * _one_plus_tanh(gates[:, d_model:2 * d_model]) * cproj


def _sample_attn_kernel(q_ref, kn_ref, vn_ref, ckt_ref, cvt_ref, sink_ref, o_ref, kwint_ref, vwint_ref, *, steps):
    n_seq = ckt_ref.shape[0]
    wc = ckt_ref.shape[2]
    half8 = _lane_half((steps, LANES))
    n_rows = N_HEADS * steps
    row_t = lax.broadcasted_iota(jnp.int32, (n_rows, 2 * wc), 0) % steps
    col = lax.broadcasted_iota(jnp.int32, (n_rows, 2 * wc), 1)
    new_t = col - (2 * wc - steps)
    valid = jnp.where(col < wc, col - row_t - (wc - WINDOW) - 1, jnp.minimum(new_t, row_t - new_t)) >= 0
    sink = sink_ref[:, 0:1]
    lane_w = lax.broadcasted_iota(jnp.int32, (LANES, wc), 1)
    keep_old = lane_w < (wc - steps)
    pad = jnp.zeros((wc - steps, LANES), F32)

    zeros_q = jnp.zeros((n_rows, LANES), BF16)
    zeros_p = jnp.zeros((n_rows, 2 * wc), BF16)

    def block_diag(pieces, zero):
        g = len(pieces)
        return jnp.concatenate(
            [jnp.concatenate([pieces[n] if m == n else zero for m in range(g)], axis=1) for n in range(g)], axis=0)

    def group(g, carry):
        seqs = [g * ATT_UNROLL + n for n in range(ATT_UNROLL)]
        rows = [pl.ds(pl.multiple_of(s * steps, steps), steps) for s in seqs]
        lhs, keys_t, vals_t, new_kt, new_vt, old_kt, old_vt = [], [], [], [], [], [], []
        for s, rw in zip(seqs, rows):
            qs = q_ref[rw, :]
            blocks = []
            for c in range(ATT_WIDTH // LANES):
                qc = qs[:, c * LANES:(c + 1) * LANES]
                qc_sw = _swap_halves(qc)
                h = c // 2
                for a in range(2):
                    blocks.append(jnp.where(half8 == h, qc if a == h else qc_sw, 0.0))
            lhs.append(_bf(jnp.concatenate(blocks, axis=0)))
            kct, vct = ckt_ref[s], cvt_ref[s]
            knt = jnp.concatenate([pad, kn_ref[rw, :]], axis=0).T
            vnt = jnp.concatenate([pad, vn_ref[rw, :]], axis=0).T
            old_kt.append(kct)
            old_vt.append(vct)
            new_kt.append(knt)
            new_vt.append(vnt)
            keys_t.append(_bf(jnp.concatenate([kct, knt], axis=1)))
            vals_t.append(_bf(jnp.concatenate([vct, vnt], axis=1)))
        sc_all = jnp.dot(block_diag(lhs, zeros_q), jnp.concatenate(keys_t, axis=0), preferred_element_type=F32)
        probs, inv = [], []
        for n in range(ATT_UNROLL):
            e, r = _softmax_with_sink(jnp.where(valid, sc_all[n * n_rows:(n + 1) * n_rows], NEG), sink)
            probs.append(_bf(e))
            inv.append(r)
        out_all = _mm_t(block_diag(probs, zeros_p), jnp.concatenate(vals_t, axis=1))
        for n, (s, rw) in enumerate(zip(seqs, rows)):
            out = out_all[n * n_rows:(n + 1) * n_rows] * inv[n]
            chunks = []
            for c in range(ATT_WIDTH // LANES):
                h = c // 2
                lo = out[(2 * c) * steps:(2 * c + 1) * steps]
                hi = out[(2 * c + 1) * steps:(2 * c + 2) * steps]
                lo = lo if h == 0 else _swap_halves(lo)
                hi = hi if h == 1 else _swap_halves(hi)
                chunks.append(jnp.where(half8 == 0, lo, hi))
            o_ref[rw, :] = jnp.concatenate(chunks, axis=1)
            kwint_ref[s] = jnp.where(keep_old, pltpu.roll(old_kt[n], wc - steps, 1), new_kt[n])
            vwint_ref[s] = jnp.where(keep_old, pltpu.roll(old_vt[n], wc - steps, 1), new_vt[n])
        return carry

    lax.fori_loop(0, n_seq // ATT_UNROLL, group, 0)


def _sample_ffn_kernel(o_ref, ga_ref, ct_ref, x_ref, p_ref, st_ref, w_att_ref, w_out_ref, g1_ref, b1_ref,
                       w_up_ref, w_fc_ref, b_fc_ref, w_dn_ref, g2_ref, b2_ref, wpg_ref, wpp_ref, g3_ref, b3_ref,
                       y_ref, fst_ref, *, alpha):
    nb, steps, d_model = x_ref.shape
    d_ff = w_dn_ref.shape[0]
    n_taps = w_fc_ref.shape[0]
    n_hist = n_taps - 1

    o = _rows_time_major(o_ref)
    x = _rows_time_major(x_ref)
    p = _rows_time_major(p_ref)
    att = _mm(o, w_att_ref[...])
    mixed = _mm(ga_ref[...] * att + ct_ref[...], w_out_ref[...])
    x1 = _layer_norm(alpha * x + mixed, g1_ref[...], b1_ref[...])
    xb = _bf(x1)

    def up_proj(c0):
        return [jnp.dot(xb, w_up_ref[:, base:base + FF_CHUNK], preferred_element_type=F32) for base in (c0, d_ff + c0)]

    def conv_act(c0, ups):
        halves = []
        for base, up in zip((c0, d_ff + c0), ups):
            cols = slice(base, base + FF_CHUNK)
            seq = [st_ref[:, r, cols] for r in range(n_hist)]
            seq += [up[t * nb:(t + 1) * nb] for t in range(steps)]
            outs = []
            for t in range(steps):
                acc = jnp.broadcast_to(b_fc_ref[:, cols], (nb, FF_CHUNK))
                for j in range(n_taps):
                    acc = acc + seq[t + j] * w_fc_ref[j:j + 1, cols]
                outs.append(acc)
            for r in range(n_hist):
                fst_ref[:, r, cols] = seq[steps + r]
            halves.append(jnp.concatenate(outs, axis=0))
        return _bf(_gelu(halves[0]) * halves[1])

    starts = list(range(0, d_ff, FF_CHUNK))
    ups = up_proj(starts[0])
    f = None
    acts = []
    for n, c0 in enumerate(starts):
        nxt = up_proj(starts[n + 1]) if n + 1 < len(starts) else None
        acts.append(conv_act(c0, ups))
        if len(acts) == DOWN_GROUP or nxt is None:
            g0 = c0 + FF_CHUNK - len(acts) * FF_CHUNK
            part = jnp.dot(jnp.concatenate(acts, axis=1), w_dn_ref[g0:c0 + FF_CHUNK, :], preferred_element_type=F32)
            f = part if f is None else f + part
            acts = []
        ups = nxt

    e_proj = _mm(p, wpp_ref[...])
    y, = _ffn_tail_rows(x1, f, e_proj, g2_ref[...], b2_ref[...], wpg_ref[...], g3_ref[...], b3_ref[...], alpha,
                        [slice(0, steps * nb)])
    for t in range(steps):
        y_ref[:, t, :] = y[t * nb:(t + 1) * nb]


def _const_spec(shape):
    zeros = (0,) * len(shape)
    return pl.BlockSpec(shape, lambda i: zeros, pipeline_mode=pl.Buffered(1))


def _const_out_spec(shape):
    zeros = (0,) * len(shape)
    return pl.BlockSpec(shape, lambda i: zeros)


def _row_spec(rows, width):
    return pl.BlockSpec((rows, width), lambda i: (i, 0))


def _params():
    return pltpu.CompilerParams(dimension_semantics=("arbitrary",), vmem_limit_bytes=VMEM_LIMIT)


def _rope_rows():
    half = HEAD_DIM // 2
    inv = ROPE_THETA ** (-jnp.arange(half, dtype=F32) / half)
    return jnp.broadcast_to(jnp.tile(inv, LANES // half)[None, :], (8, LANES))


def _row2d(v):
    return v.reshape(1, -1)


def _slice_spec(rows, cols, n_steps):
    for held in (1, 2, 4, 8):
        if (rows * held) % n_steps == 0 and (rows * held // n_steps) % 16 == 0:
            return pl.BlockSpec((rows * held // n_steps, cols), lambda i, held=held: (i // held, 0))
    raise ValueError(f"cannot slice {rows} rows over {n_steps} steps")


def _prompt_layer(x, p, rope_rows, w, late_f32, alpha):
    t_len, d_model = x.shape
    n_taps, c_conv = w["w_dw"].shape
    d_ff = late_f32["w_down"].shape[0]
    f_taps = w["w_fconv"].shape[0]
    conv_tail = -(-(n_taps - 1) // 8) * 8
    ffn_tail = -(-(f_taps - 1) // 8) * 8
    grid = (t_len // TM,)
    late_names = list(late_f32)
    late_specs = [_slice_spec(*late_f32[k].shape, grid[0]) for k in late_names]

    mixer_in = [w["sinks"], rope_rows, x, w["w_in"], w["w_attn_out"], w["w_dw"], _row2d(w["b_dw"]),
                _row2d(w["conv_ln_g"]), _row2d(w["conv_ln_b"]), w["w_conv_out"], w["w_out"],
                _row2d(w["ln1_g"]), _row2d(w["ln1_b"])]
    mixer_specs = [pl.BlockSpec(memory_space=pltpu.SMEM), _const_spec(rope_rows.shape), _row_spec(TM, d_model)]
    mixer_specs += [_const_spec(a.shape) for a in mixer_in[3:]]
    x1, kwin, vwin, cst, *late_bf16 = pl.pallas_call(
        functools.partial(_prompt_mixer_kernel, alpha=alpha, n_late=len(late_names)),
        grid=grid,
        in_specs=mixer_specs + late_specs,
        out_specs=[_row_spec(TM, d_model), _const_out_spec((WINDOW, LANES)), _const_out_spec((WINDOW, LANES)),
                   _const_out_spec((n_taps - 1, c_conv))] + late_specs,
        out_shape=[jax.ShapeDtypeStruct((t_len, d_model), F32), jax.ShapeDtypeStruct((WINDOW, LANES), F32),
                   jax.ShapeDtypeStruct((WINDOW, LANES), F32), jax.ShapeDtypeStruct((n_taps - 1, c_conv), F32)]
                  + [jax.ShapeDtypeStruct(late_f32[k].shape, BF16) for k in late_names],
        scratch_shapes=[pltpu.VMEM((WINDOW, LANES), BF16), pltpu.VMEM((WINDOW, LANES), BF16),
                        pltpu.VMEM((WINDOW, 2 * LANES), BF16),
                        pltpu.VMEM((c_conv // LANES, TM + conv_tail, LANES), F32),
                        pltpu.VMEM((TM, c_conv), F32),
                        pltpu.VMEM((2, TM, LANES), F32)],
        compiler_params=_params(),
        name="prompt_mixer",
    )(*mixer_in, *[late_f32[k] for k in late_names])
    w = dict(w, **dict(zip(late_names, late_bf16)))

    ffn_in = [x1, p, w["w_up"], w["w_fconv"], _row2d(w["b_fconv"]), w["w_down"], _row2d(w["ln2_g"]), _row2d(w["ln2_b"]),
              w["w_ple_gate"], w["w_ple_proj"], _row2d(w["ln3_g"]), _row2d(w["ln3_b"])]
    ffn_specs = [_row_spec(TM, d_model), _row_spec(TM, p.shape[1])] + [_const_spec(a.shape) for a in ffn_in[2:]]
    y, fst = pl.pallas_call(
        functools.partial(_prompt_ffn_kernel, alpha=alpha),
        grid=grid,
        in_specs=ffn_specs,
        out_specs=[_row_spec(TM, d_model), _const_out_spec((f_taps - 1, 2 * d_ff))],
        out_shape=[jax.ShapeDtypeStruct((t_len, d_model), F32), jax.ShapeDtypeStruct((f_taps - 1, 2 * d_ff), F32)],
        scratch_shapes=[pltpu.VMEM((2 * d_ff // LANES, TM + ffn_tail, LANES), F32)],
        compiler_params=_params(),
        name="prompt_ffn",
    )(*ffn_in)
    return y, kwin, vwin, cst, fst, dict(zip(late_names, late_bf16))


def _sample_layer(x, p, cache_kt, cache_vt, st_conv, st_ffn, rope_rows, w, alpha):
    n_seq, steps, d_model = x.shape
    wc = cache_kt.shape[2]
    n_tok = n_seq * steps
    blk_rows = SEQ_BLK * steps
    grid = (n_seq // SEQ_BLK,)
    seq_spec = lambda mid, width: pl.BlockSpec((SEQ_BLK, mid, width), lambda i: (i, 0, 0))
    hist_spec = pl.BlockSpec((st_conv.shape[0], SEQ_BLK, st_conv.shape[2]), lambda i: (0, i, 0))

    proj_in = [rope_rows, x, st_conv, w["w_in"], w["w_dw"], _row2d(w["b_dw"]), _row2d(w["conv_ln_g"]),
               _row2d(w["conv_ln_b"]), w["w_conv_out"]]
    proj_specs = [_const_spec(rope_rows.shape), seq_spec(steps, d_model), hist_spec]
    proj_specs += [_const_spec(a.shape) for a in proj_in[3:]]
    q, kn, vn, cst, ga, ct = pl.pallas_call(
        _sample_proj_kernel,
        grid=grid,
        in_specs=proj_specs,
        out_specs=[seq_spec(steps, ATT_WIDTH), seq_spec(steps, KV_WIDTH), seq_spec(steps, KV_WIDTH), hist_spec,
                   _row_spec(blk_rows, d_model), _row_spec(blk_rows, d_model)],
        out_shape=[jax.ShapeDtypeStruct((n_seq, steps, ATT_WIDTH), F32), jax.ShapeDtypeStruct((n_seq, steps, KV_WIDTH), F32),
                   jax.ShapeDtypeStruct((n_seq, steps, KV_WIDTH), F32), jax.ShapeDtypeStruct(st_conv.shape, F32),
                   jax.ShapeDtypeStruct((n_tok, d_model), F32), jax.ShapeDtypeStruct((n_tok, d_model), F32)],
        compiler_params=_params(),
        name="sample_proj",
    )(*proj_in)

    sink_rows = jnp.broadcast_to(jnp.repeat(w["sinks"], steps)[:, None], (N_HEADS * steps, LANES))
    att_rows = ATT_BLK * steps
    cache_spec = pl.BlockSpec((ATT_BLK, KV_WIDTH, wc), lambda i: (i, 0, 0))
    o, kwin_t, vwin_t = pl.pallas_call(
        functools.partial(_sample_attn_kernel, steps=steps),
        grid=(n_seq // ATT_BLK,),
        in_specs=[_row_spec(att_rows, ATT_WIDTH), _row_spec(att_rows, KV_WIDTH), _row_spec(att_rows, KV_WIDTH),
                  cache_spec, cache_spec, _const_spec(sink_rows.shape)],
        out_specs=[_row_spec(att_rows, ATT_WIDTH), cache_spec, cache_spec],
        out_shape=[jax.ShapeDtypeStruct((n_tok, ATT_WIDTH), F32), jax.ShapeDtypeStruct(cache_kt.shape, F32),
                   jax.ShapeDtypeStruct(cache_vt.shape, F32)],
        compiler_params=_params(),
        name="sample_attn",
    )(q.reshape(n_tok, ATT_WIDTH), kn.reshape(n_tok, KV_WIDTH), vn.reshape(n_tok, KV_WIDTH), cache_kt, cache_vt, sink_rows)

    ffn_in = [o.reshape(n_seq, steps, ATT_WIDTH), ga, ct, x, p, st_ffn, w["w_attn_out"], w["w_out"], _row2d(w["ln1_g"]),
              _row2d(w["ln1_b"]), w["w_up"], w["w_fconv"], _row2d(w["b_fconv"]), w["w_down"], _row2d(w["ln2_g"]),
              _row2d(w["ln2_b"]), w["w_ple_gate"], w["w_ple_proj"], _row2d(w["ln3_g"]), _row2d(w["ln3_b"])]
    ffn_specs = [seq_spec(steps, ATT_WIDTH), _row_spec(blk_rows, d_model), _row_spec(blk_rows, d_model),
                 seq_spec(steps, d_model), seq_spec(steps, p.shape[2]), seq_spec(*st_ffn.shape[1:])]
    ffn_specs += [_const_spec(a.shape) for a in ffn_in[6:]]
    y, fst = pl.pallas_call(
        functools.partial(_sample_ffn_kernel, alpha=alpha),
        grid=grid,
        in_specs=ffn_specs,
        out_specs=[seq_spec(steps, d_model), seq_spec(*st_ffn.shape[1:])],
        out_shape=[jax.ShapeDtypeStruct(x.shape, F32), jax.ShapeDtypeStruct(st_ffn.shape, F32)],
        compiler_params=_params(),
        name="sample_ffn",
    )(*ffn_in)
    return y, kwin_t, vwin_t, cst, fst


def kernel(x_prompt, x_sample, cache_k, cache_v, state_conv, state_ffn_conv, p_prompt, p_sample, w_in, sinks, w_attn_out, w_dw, b_dw, conv_ln_g, conv_ln_b, w_conv_out, w_out, ln1_g, ln1_b, w_up, w_fconv, b_fconv, w_down, ln2_g, ln2_b, w_ple_gate, w_ple_proj, ln3_g, ln3_b):
    depth = w_in.shape[0]
    bp, t_len, d_model = x_prompt.shape
    n_seq, steps, _ = x_sample.shape
    wc = cache_k.shape[2]
    assert bp == 1 and t_len % TM == 0 and n_seq % SEQ_BLK == 0 and n_seq % ATT_BLK == 0 and ATT_BLK % ATT_UNROLL == 0
    assert KV_WIDTH == LANES and wc == WINDOW and wc == LANES and steps == 8
    assert cache_k.shape[3:] == (N_KV_HEADS, HEAD_DIM)
    alpha = (2 * depth) ** 0.25
    rope_rows = _rope_rows()
    c_conv = w_dw.shape[2]
    qkv_cols = _in_proj_columns(d_model, c_conv)[0]
    gated_col_scale = jnp.where(jnp.arange(w_in.shape[2]) < qkv_cols[1], 1.0, 0.5).astype(F32)

    matmul_weights = dict(w_in=w_in, w_attn_out=w_attn_out, w_conv_out=w_conv_out, w_out=w_out)
    late_weights = dict(w_up=w_up, w_down=w_down, w_ple_gate=w_ple_gate, w_ple_proj=w_ple_proj)
    other = dict(sinks=sinks, w_dw=w_dw, b_dw=b_dw, conv_ln_g=conv_ln_g, conv_ln_b=conv_ln_b, ln1_g=ln1_g, ln1_b=ln1_b,
                 w_fconv=w_fconv, b_fconv=b_fconv, ln2_g=ln2_g, ln2_b=ln2_b, ln3_g=ln3_g, ln3_b=ln3_b)

    def keys_on_lanes(c):
        return jnp.transpose(c, (0, 2, 3, 1)).reshape(n_seq, KV_WIDTH, wc)

    def keys_on_rows(ct):
        return jnp.transpose(ct.reshape(n_seq, N_KV_HEADS, HEAD_DIM, wc), (0, 3, 1, 2))

    yp = x_prompt.reshape(t_len, d_model)
    ys = x_sample
    outs = [[] for _ in range(8)]
    for l in range(depth):
        w = {name: _bf(a[l]) for name, a in matmul_weights.items()}
        w["w_in"] = _bf(w_in[l] * gated_col_scale)
        w.update({name: a[l] for name, a in other.items()})
        yp, kp, vp, cp, fp, late_bf16 = _prompt_layer(yp, p_prompt[l, 0], rope_rows, w,
                                                      {name: a[l] for name, a in late_weights.items()}, alpha)
        w.update(late_bf16)
        ys, ks_t, vs_t, cs, fs = _sample_layer(
            ys, p_sample[l], keys_on_lanes(cache_k[l]), keys_on_lanes(cache_v[l]),
            jnp.transpose(state_conv[l], (1, 0, 2)), state_ffn_conv[l], rope_rows, w, alpha)
        kv_shape = (N_KV_HEADS, HEAD_DIM)
        for lst, a in zip(outs, (kp.reshape(1, wc, *kv_shape), vp.reshape(1, wc, *kv_shape), cp[None], fp[None],
                                 keys_on_rows(ks_t), keys_on_rows(vs_t), jnp.transpose(cs, (1, 0, 2)), fs)):
            lst.append(a)
    return (yp.reshape(x_prompt.shape), ys) + tuple(jnp.stack(lst) for lst in outs)
```

```python
import functools
import math

import jax
import jax.numpy as jnp
from jax import lax
from jax.experimental import pallas as pl
from jax.experimental.pallas import tpu as pltpu

F32 = jnp.float32
BF16 = jnp.bfloat16

LANES = 128
HEAD_DIM = 64
N_HEADS = 8
N_KV_HEADS = 2
ATT_WIDTH = N_HEADS * HEAD_DIM
KV_WIDTH = N_KV_HEADS * HEAD_DIM
WINDOW = 128
ROPE_THETA = 10000.0
PAST_LEN = 16384
LN_EPS = 1e-5
NEG = -1e30
GELU_C = math.sqrt(2.0 / math.pi)

TM = 512
SEQ_BLK = 32
ATT_UNROLL = 4
FF_CHUNK = 256
DOWN_GROUP = 4
CONV_ROWS = 128
VMEM_LIMIT = 56 * 1024 * 1024


def _in_proj_columns(d_model, c_conv):
    qkv = ATT_WIDTH + 2 * KV_WIDTH
    glu = qkv + 2 * c_conv
    return (0, qkv), (qkv, glu), (glu, glu + d_model), (glu + d_model, glu + 2 * d_model)


def _bf(x):
    return x.astype(BF16)


def _mm(a, w):
    return jnp.dot(_bf(a), w, preferred_element_type=F32)


def _mm_t(a, b):
    return lax.dot_general(a, b, (((1,), (1,)), ((), ())), preferred_element_type=F32)


def _sigmoid(x):
    return 0.5 * (jnp.tanh(0.5 * x) + 1.0)


def _one_plus_tanh(h):
    return jnp.tanh(h) + 1.0


def _gelu(x):
    return 0.5 * x * (1.0 + jnp.tanh(GELU_C * (x + 0.044715 * (x * x * x))))


def _layer_norm(x, g, b):
    mu = jnp.mean(x, axis=-1, keepdims=True)
    d = x - mu
    var = jnp.mean(d * d, axis=-1, keepdims=True)
    return d * lax.rsqrt(var + LN_EPS) * g + b


def _lane_half(shape):
    return (lax.broadcasted_iota(jnp.int32, shape, len(shape) - 1) // HEAD_DIM) % 2


def _first_half_of_head(shape):
    lane = lax.broadcasted_iota(jnp.int32, shape, len(shape) - 1)
    return (lane % HEAD_DIM) < (HEAD_DIM // 2)


def _rope(x, cos, sin_signed):
    partner = jnp.where(_first_half_of_head(x.shape), pltpu.roll(x, LANES - HEAD_DIM // 2, 1),
                        pltpu.roll(x, HEAD_DIM // 2, 1))
    return x * cos + partner * sin_signed


def _swap_halves(x):
    return pltpu.roll(x, HEAD_DIM, 1)


def _softmax_with_sink(s, sink):
    m = jnp.maximum(jnp.max(s, axis=-1, keepdims=True), sink)
    e = jnp.exp(s - m)
    den = jnp.sum(e, axis=-1, keepdims=True) + jnp.exp(sink - m)
    return e, 1.0 / den


def _ffn_tail_rows(x1, f, e_proj, ln2_g, ln2_b, wpg, ln3_g, ln3_b, alpha, row_groups):
    x2 = []
    gate = []
    for rows in row_groups:
        x2.append(_layer_norm(alpha * x1[rows] + f[rows], ln2_g, ln2_b))
        gate.append(_mm(x2[-1], wpg))
    return [_layer_norm(alpha * x2[n] + _sigmoid(gate[n]) * e_proj[rows], ln3_g, ln3_b)
            for n, rows in enumerate(row_groups)]


def _prompt_mixer_kernel(sinks_ref, rope_ref, x_ref, w_in_ref, w_att_ref, w_dw_ref, b_dw_ref,
                         cg_ref, cb_ref, w_co_ref, w_out_ref, g1_ref, b1_ref, *rest, alpha, n_late):
    late_f32, rest = rest[:n_late], rest[n_late:]
    (x1_ref, kwin_ref, vwin_ref, cst_ref), rest = rest[:4], rest[4:]
    late_bf16, (kprev, kprev_sw, vprev2, useq, cbuf, rtab) = rest[:n_late], rest[n_late:]
    i = pl.program_id(0)

    for src, dst in zip(late_f32, late_bf16):
        dst[...] = _bf(src[...])

    tail = useq.shape[1] - TM
    n_grp = useq.shape[0]
    n_taps = w_dw_ref.shape[0]
    first = tail - (n_taps - 1)
    d_model = x_ref.shape[1]
    c_conv = n_grp * LANES
    qkv_cols, glu_cols, ga_cols, gc_cols = _in_proj_columns(d_model, c_conv)
    inv_freq = rope_ref[0:1, :]
    n_blk = TM // WINDOW

    @pl.when(i == 0)
    def _():
        kprev[...] = jnp.zeros_like(kprev)
        kprev_sw[...] = jnp.zeros_like(kprev_sw)
        vprev2[...] = jnp.zeros_like(vprev2)
        useq[:, TM:TM + tail, :] = jnp.zeros((n_grp, tail, LANES), F32)
        ang = lax.broadcasted_iota(jnp.int32, (TM, LANES), 0).astype(F32) * inv_freq
        rtab[0] = jnp.cos(ang)
        rtab[1] = jnp.sin(ang)

    useq[:, 0:tail, :] = useq[:, TM:TM + tail, :]

    x = x_ref[...]
    xb = _bf(x)

    ang_b = (i * TM).astype(F32) * inv_freq
    cos_b, sin_b = jnp.cos(ang_b), jnp.sin(ang_b)
    cos_r, sin_r = rtab[0], rtab[1]
    cos = cos_b * cos_r - sin_b * sin_r
    sin = sin_b * cos_r + cos_b * sin_r
    sin = jnp.where(_first_half_of_head((TM, LANES)), -sin, sin)

    half_c = c_conv // 2

    def glu_proj(hh):
        lo = glu_cols[0] + hh * half_c
        return (jnp.dot(xb, w_in_ref[:, lo:lo + half_c], preferred_element_type=F32),
                jnp.dot(xb, w_in_ref[:, lo + c_conv:lo + c_conv + half_c], preferred_element_type=F32))

    def glu_to_history(hh, ab):
        u = ab[0] * _one_plus_tanh(ab[1])
        for cc in range(half_c // LANES):
            useq[hh * (half_c // LANES) + cc, tail:tail + TM, :] = u[:, cc * LANES:(cc + 1) * LANES]

    def conv_group(c):
        cols = slice(c * LANES, (c + 1) * LANES)
        for r0 in range(0, TM, CONV_ROWS):
            acc = jnp.broadcast_to(b_dw_ref[:, cols], (CONV_ROWS, LANES))
            for j in range(n_taps):
                acc = acc + useq[c, pl.ds(first + j + r0, CONV_ROWS), :] * w_dw_ref[j:j + 1, cols]
            cbuf[r0:r0 + CONV_ROWS, cols] = acc

    glu0 = glu_proj(0)
    glu1 = glu_proj(1)
    glu_to_history(0, glu0)
    qkv = jnp.dot(xb, w_in_ref[:, qkv_cols[0]:qkv_cols[1]], preferred_element_type=F32)
    for c in range(0, n_grp // 2):
        conv_group(c)
    glu_to_history(1, glu1)
    for c in range(n_grp // 2, n_grp):
        conv_group(c)

    scale = HEAD_DIM ** -0.5
    q = [_rope(qkv[:, c * LANES:(c + 1) * LANES], cos, sin) * scale for c in range(ATT_WIDTH // LANES)]
    k = _rope(qkv[:, ATT_WIDTH:ATT_WIDTH + KV_WIDTH], cos, sin)
    v = qkv[:, ATT_WIDTH + KV_WIDTH:ATT_WIDTH + 2 * KV_WIDTH]
    kb, kb_sw = _bf(k), _bf(_swap_halves(k))
    vb2 = jnp.concatenate([_bf(v), _bf(_swap_halves(v))], axis=1)

    half_q = _lane_half((WINDOW, LANES))
    own_block = (lax.broadcasted_iota(jnp.int32, (WINDOW, WINDOW), 1)
                 <= lax.broadcasted_iota(jnp.int32, (WINDOW, WINDOW), 0))
    stack_same = [(c, a) for c in range(4) for a in range(2) if a == c // 2]
    stack_swap = [(c, a) for c in range(4) for a in range(2) if a != c // 2]
    gc_step = (gc_cols[1] - gc_cols[0]) // n_blk

    o_blocks = []
    gate_conv, gate_att = [], []
    for bi in range(n_blk):
        r0 = bi * WINDOW
        if bi == 0:
            kp, kp_sw, vp2 = kprev[...], kprev_sw[...], vprev2[...]
        else:
            kp, kp_sw, vp2 = kb[r0 - WINDOW:r0], kb_sw[r0 - WINDOW:r0], vb2[r0 - WINDOW:r0]
        kk = jnp.concatenate([kp, kb[r0:r0 + WINDOW]], axis=0)
        kk_sw = jnp.concatenate([kp_sw, kb_sw[r0:r0 + WINDOW]], axis=0)
        vv2 = jnp.concatenate([vp2, vb2[r0:r0 + WINDOW]], axis=0)
        scores = []
        for heads, keys in ((stack_same, kk), (stack_swap, kk_sw)):
            qm = jnp.concatenate(
                [_bf(jnp.where(half_q == a, q[c][r0:r0 + WINDOW], 0.0)) for c, a in heads], axis=0)
            scores.append(_mm_t(qm, keys))
        head_out = {}
        for heads, s_all in ((stack_same, scores[0]), (stack_swap, scores[1])):
            probs, inv = [], []
            for n, (c, a) in enumerate(heads):
                s_prev = s_all[n * WINDOW:(n + 1) * WINDOW, 0:WINDOW]
                if bi == 0:
                    s_prev = jnp.where(i == 0, NEG, s_prev)
                s = jnp.where(own_block, s_all[n * WINDOW:(n + 1) * WINDOW, WINDOW:2 * WINDOW], s_prev)
                e, r = _softmax_with_sink(s, sinks_ref[2 * c + a])
                probs.append(_bf(jnp.concatenate([jnp.where(own_block, 0.0, e), jnp.where(own_block, e, 0.0)], axis=1)))
                inv.append(r)
            gates, first_col = (gate_conv, gc_cols[0]) if heads is stack_same else (gate_att, ga_cols[0])
            lo_col = first_col + bi * gc_step
            gates.append(jnp.dot(xb, w_in_ref[:, lo_col:lo_col + gc_step], preferred_element_type=F32))
            pv = jnp.dot(jnp.concatenate(probs, axis=0), vv2, preferred_element_type=F32)
            for n, (c, a) in enumerate(heads):
                head_out[(c, a)] = (pv[n * WINDOW:(n + 1) * WINDOW], inv[n])
        chunks = []
        for c in range(4):
            h = c // 2
            (pv_lo, inv_lo), (pv_hi, inv_hi) = head_out[(c, 0)], head_out[(c, 1)]
            lo = pv_lo[:, (0 if h == 0 else LANES):(LANES if h == 0 else 2 * LANES)] * inv_lo
            hi = pv_hi[:, (0 if h == 1 else LANES):(LANES if h == 1 else 2 * LANES)] * inv_hi
            chunks.append(jnp.where(half_q == 0, lo, hi))
        o_blocks.append(jnp.concatenate(chunks, axis=1))
    o = jnp.concatenate(o_blocks, axis=0)
    att = _mm(o, w_att_ref[...])

    kprev[...] = kb[TM - WINDOW:]
    kprev_sw[...] = kb_sw[TM - WINDOW:]
    vprev2[...] = vb2[TM - WINDOW:]
    kwin_ref[...] = k[TM - WINDOW:]
    vwin_ref[...] = v[TM - WINDOW:]
    for c in range(n_grp):
        cst_ref[:, c * LANES:(c + 1) * LANES] = useq[c, pl.ds(TM + first, n_taps - 1), :]

    cn = _layer_norm(cbuf[...], cg_ref[...], cb_ref[...])
    cproj = _mm(cn * _sigmoid(cn), w_co_ref[...])

    gate_conv = jnp.concatenate(gate_conv, axis=1)
    gate_att = jnp.concatenate(gate_att, axis=1)
    row_groups = [slice(0, TM // 2), slice(TM // 2, TM)]
    mixed = []
    for rows in row_groups:
        merged2 = _one_plus_tanh(gate_att[rows]) * att[rows] + _one_plus_tanh(gate_conv[rows]) * cproj[rows]
        mixed.append(_mm(merged2, w_out_ref[...]))
    for n, rows in enumerate(row_groups):
        x1_ref[rows, :] = _layer_norm(alpha * x[rows] + 0.5 * mixed[n], g1_ref[...], b1_ref[...])


def _prompt_ffn_kernel(x1_ref, p_ref, w_up_ref, w_fc_ref, b_fc_ref, w_dn_ref, g2_ref, b2_ref,
                       wpg_ref, wpp_ref, g3_ref, b3_ref,
                       y_ref, fst_ref, upseq, *, alpha):
    i = pl.program_id(0)
    n_slab = upseq.shape[0]
    tail = upseq.shape[1] - TM
    d_ff = w_dn_ref.shape[0]
    n_taps = w_fc_ref.shape[0]

    @pl.when(i == 0)
    def _():
        upseq[:, TM:TM + tail, :] = jnp.zeros((n_slab, tail, LANES), F32)

    upseq[:, 0:tail, :] = upseq[:, TM:TM + tail, :]

    x1 = x1_ref[...]
    xb = _bf(x1)

    def up_proj(c0):
        return [jnp.dot(xb, w_up_ref[:, base:base + FF_CHUNK], preferred_element_type=F32) for base in (c0, d_ff + c0)]

    def conv_act(c0, ups):
        halves = []
        for base, up in zip((c0, d_ff + c0), ups):
            parts = []
            for s0 in range(0, FF_CHUNK, LANES):
                slab = (base + s0) // LANES
                cols = slice(base + s0, base + s0 + LANES)
                cur = up[:, s0:s0 + LANES]
                upseq[slab, tail:tail + TM, :] = cur
                hcv = b_fc_ref[:, cols] + cur * w_fc_ref[n_taps - 1:n_taps, cols]
                for j in range(n_taps - 1):
                    hcv = hcv + upseq[slab, pl.ds(tail - (n_taps - 1) + j, TM), :] * w_fc_ref[j:j + 1, cols]
                parts.append(hcv)
            halves.append(jnp.concatenate(parts, axis=1))
        return _bf(_gelu(halves[0]) * halves[1])

    starts = list(range(0, d_ff, FF_CHUNK))
    ups = up_proj(starts[0])
    f = None
    e_proj = None
    acts = []
    for n, c0 in enumerate(starts):
        if n + 1 < len(starts):
            nxt = up_proj(starts[n + 1])
        else:
            nxt = None
            e_proj = _mm(p_ref[...], wpp_ref[...])
        acts.append(conv_act(c0, ups))
        if len(acts) == DOWN_GROUP or nxt is None:
            g0 = c0 + FF_CHUNK - len(acts) * FF_CHUNK
            part = jnp.dot(jnp.concatenate(acts, axis=1), w_dn_ref[g0:c0 + FF_CHUNK, :], preferred_element_type=F32)
            f = part if f is None else f + part
            acts = []
        ups = nxt

    for s in range(n_slab):
        fst_ref[:, s * LANES:(s + 1) * LANES] = upseq[s, pl.ds(TM + tail - (n_taps - 1), n_taps - 1), :]

    row_groups = [slice(g * (TM // 4), (g + 1) * (TM // 4)) for g in range(4)]
    ys = _ffn_tail_rows(x1, f, e_proj, g2_ref[...], b2_ref[...], wpg_ref[...], g3_ref[...], b3_ref[...], alpha, row_groups)
    for rows, y in zip(row_groups, ys):
        y_ref[rows, :] = y


def _rows_time_major(ref):
    return jnp.concatenate([ref[:, t, :] for t in range(ref.shape[1])], axis=0)


def _sample_proj_kernel(rope_ref, x_ref, st_ref, ckt_ref, cvt_ref, sink_ref, w_in_ref, w_dw_ref, b_dw_ref, cg_ref, cb_ref,
                        w_co_ref, o_ref, kwint_ref, vwint_ref, cst_ref, ga_ref, ct_ref, q_ref, k_ref, v_ref):
    nb, steps, d_model = x_ref.shape
    n_taps, c_conv = w_dw_ref.shape
    n_hist = n_taps - 1
    qkv_cols, glu_cols, ga_cols, gc_cols = _in_proj_columns(d_model, c_conv)
    x = _rows_time_major(x_ref)
    xb = _bf(x)

    pos = (PAST_LEN + lax.broadcasted_iota(jnp.int32, (steps, LANES), 0)).astype(F32)
    ang = pos * rope_ref[0:1, :]
    cos8 = jnp.cos(ang)
    sin8 = jnp.where(_first_half_of_head((steps, LANES)), -jnp.sin(ang), jnp.sin(ang))
    cos = jnp.concatenate([jnp.broadcast_to(cos8[t:t + 1], (nb, LANES)) for t in range(steps)], axis=0)
    sin = jnp.concatenate([jnp.broadcast_to(sin8[t:t + 1], (nb, LANES)) for t in range(steps)], axis=0)

    qkv = jnp.dot(xb, w_in_ref[:, qkv_cols[0]:qkv_cols[1]], preferred_element_type=F32)
    scale = HEAD_DIM ** -0.5
    q = jnp.concatenate([_rope(qkv[:, c * LANES:(c + 1) * LANES], cos, sin) * scale
                         for c in range(ATT_WIDTH // LANES)], axis=1)
    k = _rope(qkv[:, ATT_WIDTH:ATT_WIDTH + KV_WIDTH], cos, sin)
    v = qkv[:, ATT_WIDTH + KV_WIDTH:ATT_WIDTH + 2 * KV_WIDTH]
    for t in range(steps):
        rows = slice(t * nb, (t + 1) * nb)
        q_ref[:, t, :] = q[rows]
        k_ref[:, t, :] = k[rows]
        v_ref[:, t, :] = v[rows]

    _attend_cached(q_ref, k_ref, v_ref, ckt_ref, cvt_ref, sink_ref, o_ref, kwint_ref, vwint_ref)

    glu = jnp.dot(xb, w_in_ref[:, glu_cols[0]:glu_cols[1]], preferred_element_type=F32)
    u = glu[:, 0:c_conv] * _one_plus_tanh(glu[:, c_conv:2 * c_conv])
    seq = [st_ref[r] for r in range(n_hist)]
    seq += [u[t * nb:(t + 1) * nb] for t in range(steps)]
    outs = []
    for t in range(steps):
        acc = jnp.broadcast_to(b_dw_ref[...], (nb, c_conv))
        for j in range(n_taps):
            acc = acc + seq[t + j] * w_dw_ref[j:j + 1, :]
        outs.append(acc)
    for r in range(n_hist):
        cst_ref[r] = seq[steps + r]
    cn = _layer_norm(jnp.concatenate(outs, axis=0), cg_ref[...], cb_ref[...])
    cproj = _mm(cn * _sigmoid(cn), w_co_ref[...])

    gates = jnp.dot(xb, w_in_ref[:, ga_cols[0]:gc_cols[1]], preferred_element_type=F32)
    ga_ref[...] = 0.5 * _one_plus_tanh(gates[:, 0:d_model])
    ct_ref[...] = 0.5 * _one_plus_tanh(gates[:, d_model:2 * d_model]) * cproj


def _attend_cached(q_ref, kn_ref, vn_ref, ckt_ref, cvt_ref, sink_ref, o_ref, kwint_ref, vwint_ref):
    steps = q_ref.shape[1]
    n_seq = ckt_ref.shape[0]
    wc = ckt_ref.shape[2]
    half8 = _lane_half((steps, LANES))
    n_rows = N_HEADS * steps
    row_t = lax.broadcasted_iota(jnp.int32, (n_rows, 2 * wc), 0) % steps
    col = lax.broadcasted_iota(jnp.int32, (n_rows, 2 * wc), 1)
    new_t = col - (2 * wc - steps)
    valid = jnp.where(col < wc, col - row_t - (wc - WINDOW) - 1, jnp.minimum(new_t, row_t - new_t)) >= 0
    sink = sink_ref[:, 0:1]
    lane_w = lax.broadcasted_iota(jnp.int32, (LANES, wc), 1)
    keep_old = lane_w < (wc - steps)
    pad = jnp.zeros((wc - steps, LANES), F32)

    zeros_q = jnp.zeros((n_rows, LANES), BF16)
    zeros_p = jnp.zeros((n_rows, 2 * wc), BF16)

    def block_diag(pieces, zero):
        g = len(pieces)
        return jnp.concatenate(
            [jnp.concatenate([pieces[n] if m == n else zero for m in range(g)], axis=1) for n in range(g)], axis=0)

    def group(g, carry):
        seqs = [g * ATT_UNROLL + n for n in range(ATT_UNROLL)]
        lhs, keys_t, vals_t, new_kt, new_vt, old_kt, old_vt = [], [], [], [], [], [], []
        for s in seqs:
            qs = q_ref[s]
            blocks = []
            for c in range(ATT_WIDTH // LANES):
                qc = qs[:, c * LANES:(c + 1) * LANES]
                qc_sw = _swap_halves(qc)
                h = c // 2
                for a in range(2):
                    blocks.append(jnp.where(half8 == h, qc if a == h else qc_sw, 0.0))
            lhs.append(_bf(jnp.concatenate(blocks, axis=0)))
            kct, vct = ckt_ref[s], cvt_ref[s]
            knt = jnp.concatenate([pad, kn_ref[s]], axis=0).T
            vnt = jnp.concatenate([pad, vn_ref[s]], axis=0).T
            old_kt.append(kct)
            old_vt.append(vct)
            new_kt.append(knt)
            new_vt.append(vnt)
            keys_t.append(_bf(jnp.concatenate([kct, knt], axis=1)))
            vals_t.append(_bf(jnp.concatenate([vct, vnt], axis=1)))
        sc_all = jnp.dot(block_diag(lhs, zeros_q), jnp.concatenate(keys_t, axis=0), preferred_element_type=F32)
        probs, inv = [], []
        for n in range(ATT_UNROLL):
            e, r = _softmax_with_sink(jnp.where(valid, sc_all[n * n_rows:(n + 1) * n_rows], NEG), sink)
            probs.append(_bf(e))
            inv.append(r)
        out_all = _mm_t(block_diag(probs, zeros_p), jnp.concatenate(vals_t, axis=1))
        for n, s in enumerate(seqs):
            out = out_all[n * n_rows:(n + 1) * n_rows] * inv[n]
            chunks = []
            for c in range(ATT_WIDTH // LANES):
                h = c // 2
                lo = out[(2 * c) * steps:(2 * c + 1) * steps]
                hi = out[(2 * c + 1) * steps:(2 * c + 2) * steps]
                lo = lo if h == 0 else _swap_halves(lo)
                hi = hi if h == 1 else _swap_halves(hi)
                chunks.append(jnp.where(half8 == 0, lo, hi))
            o_ref[s] = jnp.concatenate(chunks, axis=1)
            kwint_ref[s] = jnp.where(keep_old, pltpu.roll(old_kt[n], wc - steps, 1), new_kt[n])
            vwint_ref[s] = jnp.where(keep_old, pltpu.roll(old_vt[n], wc - steps, 1), new_vt[n])
        return carry

    lax.fori_loop(0, n_seq // ATT_UNROLL, group, 0)


def _sample_ffn_kernel(o_ref, ga_ref, ct_ref, x_ref, p_ref, st_ref, w_att_ref, w_out_ref, g1_ref, b1_ref,
                       w_up_ref, w_fc_ref, b_fc_ref, w_dn_ref, g2_ref, b2_ref, wpg_ref, wpp_ref, g3_ref, b3_ref,
                       y_ref, fst_ref, *, alpha):
    nb, steps, d_model = x_ref.shape
    d_ff = w_dn_ref.shape[0]
    n_taps = w_fc_ref.shape[0]
    n_hist = n_taps - 1

    o = _rows_time_major(o_ref)
    x = _rows_time_major(x_ref)
    p = _rows_time_major(p_ref)
    att = _mm(o, w_att_ref[...])
    mixed = _mm(ga_ref[...] * att + ct_ref[...], w_out_ref[...])
    x1 = _layer_norm(alpha * x + mixed, g1_ref[...], b1_ref[...])
    xb = _bf(x1)

    def up_proj(c0):
        return [jnp.dot(xb, w_up_ref[:, base:base + FF_CHUNK], preferred_element_type=F32) for base in (c0, d_ff + c0)]

    def conv_act(c0, ups):
        halves = []
        for base, up in zip((c0, d_ff + c0), ups):
            cols = slice(base, base + FF_CHUNK)
            seq = [st_ref[:, r, cols] for r in range(n_hist)]
            seq += [up[t * nb:(t + 1) * nb] for t in range(steps)]
            outs = []
            for t in range(steps):
                acc = jnp.broadcast_to(b_fc_ref[:, cols], (nb, FF_CHUNK))
                for j in range(n_taps):
                    acc = acc + seq[t + j] * w_fc_ref[j:j + 1, cols]
                outs.append(acc)
            for r in range(n_hist):
                fst_ref[:, r, cols] = seq[steps + r]
            halves.append(jnp.concatenate(outs, axis=0))
        return _bf(_gelu(halves[0]) * halves[1])

    starts = list(range(0, d_ff, FF_CHUNK))
    ups = up_proj(starts[0])
    f = None
    acts = []
    for n, c0 in enumerate(starts):
        nxt = up_proj(starts[n + 1]) if n + 1 < len(starts) else None
        acts.append(conv_act(c0, ups))
        if len(acts) == DOWN_GROUP or nxt is None:
            g0 = c0 + FF_CHUNK - len(acts) * FF_CHUNK
            part = jnp.dot(jnp.concatenate(acts, axis=1), w_dn_ref[g0:c0 + FF_CHUNK, :], preferred_element_type=F32)
            f = part if f is None else f + part
            acts = []
        ups = nxt

    e_proj = _mm(p, wpp_ref[...])
    y, = _ffn_tail_rows(x1, f, e_proj, g2_ref[...], b2_ref[...], wpg_ref[...], g3_ref[...], b3_ref[...], alpha,
                        [slice(0, steps * nb)])
    for t in range(steps):
        y_ref[:, t, :] = y[t * nb:(t + 1) * nb]


def _const_spec(shape):
    zeros = (0,) * len(shape)
    return pl.BlockSpec(shape, lambda i: zeros, pipeline_mode=pl.Buffered(1))


def _const_out_spec(shape):
    zeros = (0,) * len(shape)
    return pl.BlockSpec(shape, lambda i: zeros)


def _row_spec(rows, width):
    return pl.BlockSpec((rows, width), lambda i: (i, 0))


def _params():
    return pltpu.CompilerParams(dimension_semantics=("arbitrary",), vmem_limit_bytes=VMEM_LIMIT)


def _rope_rows():
    half = HEAD_DIM // 2
    inv = ROPE_THETA ** (-jnp.arange(half, dtype=F32) / half)
    return jnp.broadcast_to(jnp.tile(inv, LANES // half)[None, :], (8, LANES))


def _row2d(v):
    return v.reshape(1, -1)


def _slice_spec(rows, cols, n_steps):
    for held in (1, 2, 4, 8):
        if (rows * held) % n_steps == 0 and (rows * held // n_steps) % 16 == 0:
            return pl.BlockSpec((rows * held // n_steps, cols), lambda i, held=held: (i // held, 0))
    raise ValueError(f"cannot slice {rows} rows over {n_steps} steps")


def _prompt_layer(x, p, rope_rows, w, late_f32, alpha):
    t_len, d_model = x.shape
    n_taps, c_conv = w["w_dw"].shape
    d_ff = late_f32["w_down"].shape[0]
    f_taps = w["w_fconv"].shape[0]
    conv_tail = -(-(n_taps - 1) // 8) * 8
    ffn_tail = -(-(f_taps - 1) // 8) * 8
    grid = (t_len // TM,)
    late_names = list(late_f32)
    late_specs = [_slice_spec(*late_f32[k].shape, grid[0]) for k in late_names]

    mixer_in = [w["sinks"], rope_rows, x, w["w_in"], w["w_attn_out"], w["w_dw"], _row2d(w["b_dw"]),
                _row2d(w["conv_ln_g"]), _row2d(w["conv_ln_b"]), w["w_conv_out"], w["w_out"],
                _row2d(w["ln1_g"]), _row2d(w["ln1_b"])]
    mixer_specs = [pl.BlockSpec(memory_space=pltpu.SMEM), _const_spec(rope_rows.shape), _row_spec(TM, d_model)]
    mixer_specs += [_const_spec(a.shape) for a in mixer_in[3:]]
    x1, kwin, vwin, cst, *late_bf16 = pl.pallas_call(
        functools.partial(_prompt_mixer_kernel, alpha=alpha, n_late=len(late_names)),
        grid=grid,
        in_specs=mixer_specs + late_specs,
        out_specs=[_row_spec(TM, d_model), _const_out_spec((WINDOW, LANES)), _const_out_spec((WINDOW, LANES)),
                   _const_out_spec((n_taps - 1, c_conv))] + late_specs,
        out_shape=[jax.ShapeDtypeStruct((t_len, d_model), F32), jax.ShapeDtypeStruct((WINDOW, LANES), F32),
                   jax.ShapeDtypeStruct((WINDOW, LANES), F32), jax.ShapeDtypeStruct((n_taps - 1, c_conv), F32)]
                  + [jax.ShapeDtypeStruct(late_f32[k].shape, BF16) for k in late_names],
        scratch_shapes=[pltpu.VMEM((WINDOW, LANES), BF16), pltpu.VMEM((WINDOW, LANES), BF16),
                        pltpu.VMEM((WINDOW, 2 * LANES), BF16),
                        pltpu.VMEM((c_conv // LANES, TM + conv_tail, LANES), F32),
                        pltpu.VMEM((TM, c_conv), F32),
                        pltpu.VMEM((2, TM, LANES), F32)],
        compiler_params=_params(),
        name="prompt_mixer",
    )(*mixer_in, *[late_f32[k] for k in late_names])
    w = dict(w, **dict(zip(late_names, late_bf16)))

    ffn_in = [x1, p, w["w_up"], w["w_fconv"], _row2d(w["b_fconv"]), w["w_down"], _row2d(w["ln2_g"]), _row2d(w["ln2_b"]),
              w["w_ple_gate"], w["w_ple_proj"], _row2d(w["ln3_g"]), _row2d(w["ln3_b"])]
    ffn_specs = [_row_spec(TM, d_model), _row_spec(TM, p.shape[1])] + [_const_spec(a.shape) for a in ffn_in[2:]]
    y, fst = pl.pallas_call(
        functools.partial(_prompt_ffn_kernel, alpha=alpha),
        grid=grid,
        in_specs=ffn_specs,
        out_specs=[_row_spec(TM, d_model), _const_out_spec((f_taps - 1, 2 * d_ff))],
        out_shape=[jax.ShapeDtypeStruct((t_len, d_model), F32), jax.ShapeDtypeStruct((f_taps - 1, 2 * d_ff), F32)],
        scratch_shapes=[pltpu.VMEM((2 * d_ff // LANES, TM + ffn_tail, LANES), F32)],
        compiler_params=_params(),
        name="prompt_ffn",
    )(*ffn_in)
    return y, kwin, vwin, cst, fst, dict(zip(late_names, late_bf16))


def _sample_layer(x, p, cache_kt, cache_vt, st_conv, st_ffn, rope_rows, w, alpha):
    n_seq, steps, d_model = x.shape
    wc = cache_kt.shape[2]
    n_tok = n_seq * steps
    blk_rows = SEQ_BLK * steps
    grid = (n_seq // SEQ_BLK,)
    seq_spec = lambda mid, width: pl.BlockSpec((SEQ_BLK, mid, width), lambda i: (i, 0, 0))
    hist_spec = pl.BlockSpec((st_conv.shape[0], SEQ_BLK, st_conv.shape[2]), lambda i: (0, i, 0))

    sink_rows = jnp.broadcast_to(jnp.repeat(w["sinks"], steps)[:, None], (N_HEADS * steps, LANES))
    cache_spec = pl.BlockSpec((SEQ_BLK, KV_WIDTH, wc), lambda i: (i, 0, 0))
    proj_in = [rope_rows, x, st_conv, cache_kt, cache_vt, sink_rows, w["w_in"], w["w_dw"], _row2d(w["b_dw"]),
               _row2d(w["conv_ln_g"]), _row2d(w["conv_ln_b"]), w["w_conv_out"]]
    proj_specs = [_const_spec(rope_rows.shape), seq_spec(steps, d_model), hist_spec, cache_spec, cache_spec,
                  _const_spec(sink_rows.shape)]
    proj_specs += [_const_spec(a.shape) for a in proj_in[6:]]
    o, kwin_t, vwin_t, cst, ga, ct = pl.pallas_call(
        _sample_proj_kernel,
        grid=grid,
        in_specs=proj_specs,
        out_specs=[seq_spec(steps, ATT_WIDTH), cache_spec, cache_spec, hist_spec,
                   _row_spec(blk_rows, d_model), _row_spec(blk_rows, d_model)],
        out_shape=[jax.ShapeDtypeStruct((n_seq, steps, ATT_WIDTH), F32), jax.ShapeDtypeStruct(cache_kt.shape, F32),
                   jax.ShapeDtypeStruct(cache_vt.shape, F32), jax.ShapeDtypeStruct(st_conv.shape, F32),
                   jax.ShapeDtypeStruct((n_tok, d_model), F32), jax.ShapeDtypeStruct((n_tok, d_model), F32)],
        scratch_shapes=[pltpu.VMEM((SEQ_BLK, steps, ATT_WIDTH), F32), pltpu.VMEM((SEQ_BLK, steps, KV_WIDTH), F32),
                        pltpu.VMEM((SEQ_BLK, steps, KV_WIDTH), F32)],
        compiler_params=_params(),
        name="sample_proj_attn",
    )(*proj_in)

    ffn_in = [o, ga, ct, x, p, st_ffn, w["w_attn_out"], w["w_out"], _row2d(w["ln1_g"]),
              _row2d(w["ln1_b"]), w["w_up"], w["w_fconv"], _row2d(w["b_fconv"]), w["w_down"], _row2d(w["ln2_g"]),
              _row2d(w["ln2_b"]), w["w_ple_gate"], w["w_ple_proj"], _row2d(w["ln3_g"]), _row2d(w["ln3_b"])]
    ffn_specs = [seq_spec(steps, ATT_WIDTH), _row_spec(blk_rows, d_model), _row_spec(blk_rows, d_model),
                 seq_spec(steps, d_model), seq_spec(steps, p.shape[2]), seq_spec(*st_ffn.shape[1:])]
    ffn_specs += [_const_spec(a.shape) for a in ffn_in[6:]]
    y, fst = pl.pallas_call(
        functools.partial(_sample_ffn_kernel, alpha=alpha),
        grid=grid,
        in_specs=ffn_specs,
        out_specs=[seq_spec(steps, d_model), seq_spec(*st_ffn.shape[1:])],
        out_shape=[jax.ShapeDtypeStruct(x.shape, F32), jax.ShapeDtypeStruct(st_ffn.shape, F32)],
        compiler_params=_params(),
        name="sample_ffn",
    )(*ffn_in)
    return y, kwin_t, vwin_t, cst, fst


def kernel(x_prompt, x_sample, cache_k, cache_v, state_conv, state_ffn_conv, p_prompt, p_sample, w_in, sinks, w_attn_out, w_dw, b_dw, conv_ln_g, conv_ln_b, w_conv_out, w_out, ln1_g, ln1_b, w_up, w_fconv, b_fconv, w_down, ln2_g, ln2_b, w_ple_gate, w_ple_proj, ln3_g, ln3_b):
    depth = w_in.shape[0]
    bp, t_len, d_model = x_prompt.shape
    n_seq, steps, _ = x_sample.shape
    wc = cache_k.shape[2]
    assert bp == 1 and t_len % TM == 0 and n_seq % SEQ_BLK == 0 and SEQ_BLK % ATT_UNROLL == 0
    assert KV_WIDTH == LANES and wc == WINDOW and wc == LANES and steps == 8
    assert cache_k.shape[3:] == (N_KV_HEADS, HEAD_DIM)
    alpha = (2 * depth) ** 0.25
    rope_rows = _rope_rows()
    c_conv = w_dw.shape[2]
    qkv_cols = _in_proj_columns(d_model, c_conv)[0]
    gated_col_scale = jnp.where(jnp.arange(w_in.shape[2]) < qkv_cols[1], 1.0, 0.5).astype(F32)

    matmul_weights = dict(w_in=w_in, w_attn_out=w_attn_out, w_conv_out=w_conv_out, w_out=w_out)
    late_weights = dict(w_up=w_up, w_down=w_down, w_ple_gate=w_ple_gate, w_ple_proj=w_ple_proj)
    other = dict(sinks=sinks, w_dw=w_dw, b_dw=b_dw, conv_ln_g=conv_ln_g, conv_ln_b=conv_ln_b, ln1_g=ln1_g, ln1_b=ln1_b,
                 w_fconv=w_fconv, b_fconv=b_fconv, ln2_g=ln2_g, ln2_b=ln2_b, ln3_g=ln3_g, ln3_b=ln3_b)

    def keys_on_lanes(c):
        return jnp.transpose(c, (0, 2, 3, 1)).reshape(n_seq, KV_WIDTH, wc)

    def keys_on_rows(ct):
        return jnp.transpose(ct.reshape(n_seq, N_KV_HEADS, HEAD_DIM, wc), (0, 3, 1, 2))

    yp = x_prompt.reshape(t_len, d_model)
    ys = x_sample
    outs = [[] for _ in range(8)]
    for l in range(depth):
        w = {name: _bf(a[l]) for name, a in matmul_weights.items()}
        w["w_in"] = _bf(w_in[l] * gated_col_scale)
        w.update({name: a[l] for name, a in other.items()})
        yp, kp, vp, cp, fp, late_bf16 = _prompt_layer(yp, p_prompt[l, 0], rope_rows, w,
                                                      {name: a[l] for name, a in late_weights.items()}, alpha)
        w.update(late_bf16)
        ys, ks_t, vs_t, cs, fs = _sample_layer(
            ys, p_sample[l], keys_on_lanes(cache_k[l]), keys_on_lanes(cache_v[l]),
            jnp.transpose(state_conv[l], (1, 0, 2)), state_ffn_conv[l], rope_rows, w, alpha)
        kv_shape = (N_KV_HEADS, HEAD_DIM)
        for lst, a in zip(outs, (kp.reshape(1, wc, *kv_shape), vp.reshape(1, wc, *kv_shape), cp[None], fp[None],
                                 keys_on_rows(ks_t), keys_on_rows(vs_t), jnp.transpose(cs, (1, 0, 2)), fs)):
            lst.append(a)
    return (yp.reshape(x_prompt.shape), ys) + tuple(jnp.stack(lst) for lst in outs)
```
